```python
import math
import jax
import jax.numpy as jnp
from jax import lax
import numpy as np

D_MODEL = 1024
BATCH = 16
SEQ = 2048
DEPTH = 2

GRID_W = 64
CTX_LEN = 256
EPS = 1e-6
F32 = jnp.float32

E_HY = 512
HY_SHORT = 3
HY_EMB_BANDS = 8
HY_EMB = 1 + 2 * HY_EMB_BANDS
HY_ORDER = 64
HY_DECAY_TARGET = 1e-2
HY_FAST_DECAY = 0.3
HY_SLOW_DECAY = 1.5

E_LRU = 512
LRU_BLOCKS = 8
LRU_BLOCK_DIM = E_LRU // LRU_BLOCKS
LRU_CONV = 4
LRU_C = 8.0

HG_HEADS = 4
HG_DK = 128
HG_DV = 128
E_HG = HG_HEADS * HG_DV
HG_CHUNK = 64

D_FF = 4 * D_MODEL
N_BRANCH = 3

IN_SPLITS = (3 * E_HY, E_LRU, E_LRU, HG_HEADS * HG_DK, HG_HEADS * HG_DK, HG_HEADS * HG_DK, E_HG, E_HG, N_BRANCH * D_MODEL)
IN_WIDTH = sum(IN_SPLITS)

kernel_name = 'hybrid_hyena_rglru_hgrn2_dit_prefix'


def rmsnorm(x, g):
    xf = x.astype(F32)
    y = xf * lax.rsqrt(jnp.mean(xf * xf, axis=-1, keepdims=True) + EPS)
    return (y * g.astype(F32)).astype(x.dtype)


def modulate(x, shift, scale):
    return x * (1 + scale) + shift


def dwconv(x, w, b, pad):
    y = lax.conv_general_dilated(x, w[:, None, :].astype(x.dtype), window_strides=(1,), padding=[pad],
                                 dimension_numbers=('NWC', 'WIO', 'NWC'), feature_group_count=x.shape[-1])
    return y + b


def grid_transpose(t, n_rows, n_cols):
    bsz, length, width = t.shape
    return t.reshape(bsz, n_rows, n_cols, width).transpose(0, 2, 1, 3).reshape(bsz, length, width)


def _keep(t):
    return t


def _flip(t):
    return t[:, ::-1]


def hyena_filter(length, w1, b1, w2, b2, w3, freq):
    pos = jnp.arange(length, dtype=F32)
    t = pos / max(length - 1, 1)
    bands = jnp.linspace(1e-4, HY_EMB_BANDS - 1, HY_EMB_BANDS, dtype=F32)
    ang = bands[None, :] * (2.0 * math.pi * pos / length)[:, None]
    z = jnp.concatenate([t[:, None], jnp.cos(ang), -jnp.sin(ang)], axis=-1)
    h = jnp.sin(freq[0].astype(F32) * (z @ w1.astype(F32) + b1.astype(F32)))
    h = jnp.sin(freq[1].astype(F32) * (h @ w2.astype(F32) + b2.astype(F32)))
    h = h @ w3.astype(F32)
    deltas = jnp.abs(jnp.linspace(math.log(HY_DECAY_TARGET) / HY_SLOW_DECAY,
                                  math.log(HY_DECAY_TARGET) / HY_FAST_DECAY, E_HY, dtype=F32))
    decay = jnp.exp(-t[:, None] * deltas[None, :])
    h_fwd = h[:, :E_HY] * decay
    h_bwd = h[:, E_HY:] * decay
    return jnp.concatenate([h_fwd, jnp.zeros((1, E_HY), F32), h_bwd[1:][::-1]], axis=0)


def hyena_mixer(p_hy, short_w, short_b, w1, b1, w2, b2, w3, freq, skip):
    length = p_hy.shape[1]
    z = dwconv(p_hy, short_w, short_b, (1, 1))
    v, x0, x1 = jnp.split(z, 3, axis=-1)
    k_circ = hyena_filter(length, w1, b1, w2, b2, w3, freq)
    u = (v * x1).astype(F32)
    uf = jnp.fft.rfft(u, n=2 * length, axis=1)
    kf = jnp.fft.rfft(k_circ, axis=0)
    y = jnp.fft.irfft(uf * kf[None], n=2 * length, axis=1)[:, :length] + u * skip.astype(F32)
    return (x0.astype(F32) * y).astype(p_hy.dtype)


def _lin_combine(left, right):
    a_l, b_l = left
    a_r, b_r = right
    return a_l * a_r, a_r * b_l + b_r


def rglru_scan(xb, conv_w, conv_b, wr, br, wi, bi, lam, h0):
    bsz, length, _ = xb.shape
    xc = dwconv(xb, conv_w, conv_b, (LRU_CONV - 1, 0)).astype(F32)
    xh = xc.reshape(bsz, length, LRU_BLOCKS, LRU_BLOCK_DIM)
    r = jax.nn.sigmoid(jnp.einsum('blhi,hij->blhj', xh, wr.astype(F32)).reshape(bsz, length, E_LRU) + br.astype(F32))
    i = jax.nn.sigmoid(jnp.einsum('blhi,hij->blhj', xh, wi.astype(F32)).reshape(bsz, length, E_LRU) + bi.astype(F32))
    log_a = -LRU_C * r * jax.nn.softplus(-lam.astype(F32))
    a = jnp.exp(log_a)
    b = jnp.sqrt(-jnp.expm1(2.0 * log_a)) * (i * xc)
    b = b.at[:, 0].add(a[:, 0] * h0)
    _, h = lax.associative_scan(_lin_combine, (a, b), axis=1)
    return h, h[:, -1]


def rglru_mixer(x_ctx, g_ctx, x_lat, g_lat, conv_w, conv_b, wr, br, wi, bi, lam, need_ctx):
    bsz = x_lat.shape[0]
    h_ctx_dirs, h_lat_dirs = [], []
    for d in range(2):
        order = _flip if d == 1 else _keep
        args = (conv_w[d], conv_b[d], wr[d], br[d], wi[d], bi[d], lam[d])
        h_c, s_c = rglru_scan(order(x_ctx), *args, jnp.zeros((bsz, E_LRU), F32))
        h_l, _ = rglru_scan(order(x_lat), *args, s_c)
        h_lat_dirs.append(order(h_l))
        if need_ctx:
            h_ctx_dirs.append(order(h_c))
    y_lat = ((h_lat_dirs[0] + h_lat_dirs[1]) * jax.nn.gelu(g_lat.astype(F32))).astype(x_lat.dtype)
    y_ctx = ((h_ctx_dirs[0] + h_ctx_dirs[1]) * jax.nn.gelu(g_ctx.astype(F32))).astype(x_ctx.dtype) if need_ctx else None
    return y_ctx, y_lat


def hgrn2_chunked(q, k, v, log_f, s0, need_out):
    bsz, length = q.shape[:2]
    n = length // HG_CHUNK

    def chunks(t):
        return t.reshape(bsz, n, HG_CHUNK, HG_HEADS, t.shape[-1])

    q, k, v, log_f = chunks(q), chunks(k), chunks(v), chunks(log_f)
    b = jnp.cumsum(log_f, axis=2)
    b_last = b[:, :, -1:]
    kv = jnp.einsum('bnshk,bnshv->bnhkv', k * jnp.exp(b_last - b), v)
    decay = jnp.exp(b_last[:, :, 0])

    def step(s, inp):
        dec, kv_c = inp
        return dec[..., None] * s + kv_c, s

    s_final, s_start = lax.scan(step, s0, (jnp.moveaxis(decay, 1, 0), jnp.moveaxis(kv, 1, 0)))
    if not need_out:
        return None, s_final
    s_start = jnp.moveaxis(s_start, 0, 1)
    ref = b[:, :, HG_CHUNK // 2:HG_CHUNK // 2 + 1]
    scores = jnp.einsum('bnthk,bnshk->bnhts', q * jnp.exp(b - ref), k * jnp.exp(ref - b))
    lower_tri = jnp.tril(jnp.ones((HG_CHUNK, HG_CHUNK), dtype=bool))
    scores = jnp.where(lower_tri, scores, 0.0)
    o = jnp.einsum('bnhts,bnshv->bnthv', scores, v) + jnp.einsum('bnthk,bnhkv->bnthv', q * jnp.exp(b), s_start)
    return o.reshape(bsz, length, HG_HEADS, HG_DV), s_final


def _hgrn2_inputs(q, f_fwd, f_bwd, i, lb):
    def heads(t):
        return t.reshape(t.shape[0], t.shape[1], HG_HEADS, -1)

    qh = jax.nn.silu(heads(q.astype(F32)))
    vh = heads(i.astype(F32))
    per_dir = []
    for f_logit in (f_fwd, f_bwd):
        f = lb + (1.0 - lb) * jax.nn.sigmoid(f_logit.astype(F32))
        per_dir.append((heads(1.0 - f), heads(jnp.log(f))))
    return qh, vh, per_dir


def _hgrn2_out(o, g, norm_g):
    bsz, length = o.shape[:2]
    o = rmsnorm(o, norm_g).reshape(bsz, length, E_HG)
    return (o * jax.nn.silu(g.astype(F32))).astype(g.dtype)


def hgrn2_mixer(ctx_proj, lat_proj, lb, norm_g, need_ctx):
    q_c, v_c, dirs_c = _hgrn2_inputs(*ctx_proj[:4], lb)
    q_l, v_l, dirs_l = _hgrn2_inputs(*lat_proj[:4], lb)
    bsz = q_l.shape[0]
    s0 = jnp.zeros((bsz, HG_HEADS, HG_DK, HG_DV), F32)
    o_ctx, o_lat = [], []
    for d in range(2):
        order = _flip if d == 1 else _keep
        (k_c, lf_c), (k_l, lf_l) = dirs_c[d], dirs_l[d]
        oc, s_ctx = hgrn2_chunked(order(q_c), order(k_c), order(v_c), order(lf_c), s0, need_ctx)
        ol, _ = hgrn2_chunked(order(q_l), order(k_l), order(v_l), order(lf_l), s_ctx, True)
        o_lat.append(order(ol))
        if need_ctx:
            o_ctx.append(order(oc))
    y_lat = _hgrn2_out(o_lat[0] + o_lat[1], lat_proj[4], norm_g)
    y_ctx = _hgrn2_out(o_ctx[0] + o_ctx[1], ctx_proj[4], norm_g) if need_ctx else None
    return y_ctx, y_lat


def merge_branches(gates, y_hy, y_lru, y_hg, w_proj_hy, w_proj_lru, w_proj_hg, w_out):
    g_hy, g_lru, g_hg = jnp.split(jax.nn.sigmoid(gates), N_BRANCH, axis=-1)
    m = g_hy * (y_hy @ w_proj_hy) + g_lru * (y_lru @ w_proj_lru) + g_hg * (y_hg @ w_proj_hg)
    return m @ w_out


def sq_relu_mlp(x, w1, w2):
    return jnp.square(jax.nn.relu(x @ w1)) @ w2


def setup_inputs(seed: int = 0) -> dict:
    key = jax.random.key(seed)
    ks = iter(jax.random.split(key, 40))

    def nrm(shape, scale):
        return jax.random.normal(next(ks), shape, F32) * scale

    a0 = jax.random.uniform(next(ks), (DEPTH, 2, E_LRU), F32, minval=0.9, maxval=0.999)
    p = a0 ** (1.0 / LRU_C)
    return {
        'x': nrm((BATCH, SEQ, D_MODEL), 1.0),
        'c': nrm((BATCH, D_MODEL), 1.0),
        'ctx': nrm((BATCH, CTX_LEN, D_MODEL), 1.0),
        'c_ctx': nrm((D_MODEL,), 1.0),
        'w_ada': nrm((DEPTH, D_MODEL, 6 * D_MODEL), 0.5 * D_MODEL ** -0.5),
        'b_ada': nrm((DEPTH, 6 * D_MODEL), 0.02),
        'norm_gains': 1.0 + nrm((DEPTH, 4, D_MODEL), 0.05),
        'w_in': nrm((DEPTH, D_MODEL, IN_WIDTH), D_MODEL ** -0.5),
        'hy_short_w': nrm((DEPTH, HY_SHORT, 3 * E_HY), HY_SHORT ** -0.5),
        'hy_short_b': nrm((DEPTH, 3 * E_HY), 0.02),
        'hy_ff_w1': nrm((DEPTH, HY_EMB, HY_ORDER), HY_EMB ** -0.5),
        'hy_ff_b1': nrm((DEPTH, HY_ORDER), 0.1),
        'hy_ff_w2': nrm((DEPTH, HY_ORDER, HY_ORDER), HY_ORDER ** -0.5),
        'hy_ff_b2': nrm((DEPTH, HY_ORDER), 0.1),
        'hy_ff_w3': nrm((DEPTH, HY_ORDER, 2 * E_HY), 0.05 * HY_ORDER ** -0.5),
        'hy_freq': 1.0 + nrm((DEPTH, 2, HY_ORDER), 0.05),
        'hy_skip': nrm((DEPTH, E_HY), 0.1),
        'lru_conv_w': nrm((DEPTH, 2, LRU_CONV, E_LRU), LRU_CONV ** -0.5),
        'lru_conv_b': nrm((DEPTH, 2, E_LRU), 0.02),
        'lru_wr': nrm((DEPTH, 2, LRU_BLOCKS, LRU_BLOCK_DIM, LRU_BLOCK_DIM), LRU_BLOCK_DIM ** -0.5),
        'lru_br': nrm((DEPTH, 2, E_LRU), 0.02),
        'lru_wi': nrm((DEPTH, 2, LRU_BLOCKS, LRU_BLOCK_DIM, LRU_BLOCK_DIM), LRU_BLOCK_DIM ** -0.5),
        'lru_bi': nrm((DEPTH, 2, E_LRU), 0.02),
        'lru_lambda': jnp.log(p) - jnp.log1p(-p),
        'hg_lower_bounds': nrm((DEPTH, HG_HEADS * HG_DK), 0.1),
        'hg_norm_g': 1.0 + nrm((DEPTH, HG_DV), 0.05),
        'w_proj_hy': nrm((DEPTH, E_HY, D_MODEL), E_HY ** -0.5),
        'w_proj_lru': nrm((DEPTH, E_LRU, D_MODEL), E_LRU ** -0.5),
        'w_proj_hg': nrm((DEPTH, E_HG, D_MODEL), E_HG ** -0.5),
        'w_out': nrm((DEPTH, D_MODEL, D_MODEL), D_MODEL ** -0.5),
        'w_mlp1': nrm((DEPTH, D_MODEL, D_FF), D_MODEL ** -0.5),
        'w_mlp2': nrm((DEPTH, D_FF, D_MODEL), D_FF ** -0.5),
    }


def reference(x, c, ctx, c_ctx, w_ada, b_ada, norm_gains, w_in, hy_short_w, hy_short_b, hy_ff_w1, hy_ff_b1,
              hy_ff_w2, hy_ff_b2, hy_ff_w3, hy_freq, hy_skip, lru_conv_w, lru_conv_b, lru_wr, lru_br, lru_wi,
              lru_bi, lru_lambda, hg_lower_bounds, hg_norm_g, w_proj_hy, w_proj_lru, w_proj_hg, w_out,
              w_mlp1, w_mlp2):
    rows = x.shape[1] // GRID_W
    offs = [int(o) for o in np.cumsum(IN_SPLITS)[:-1]]
    lb_all = jnp.cumsum(jax.nn.softmax(hg_lower_bounds.astype(F32), axis=0), axis=0)
    lb_all = lb_all - lb_all[0]
    h_lat, h_ctx = x, ctx
    for layer in range(DEPTH):
        need_ctx = layer < DEPTH - 1
        col_major = layer % 2 == 1
        ada_lat = jax.nn.silu(c) @ w_ada[layer] + b_ada[layer]
        ada_ctx = jax.nn.silu(c_ctx) @ w_ada[layer] + b_ada[layer]
        sh1, sc1, g1, sh2, sc2, g2 = jnp.split(ada_lat[:, None, :], 6, axis=-1)
        sh1c, sc1c, g1c, sh2c, sc2c, g2c = jnp.split(ada_ctx, 6, axis=-1)
        gains = norm_gains[layer]

        u_lat = modulate(rmsnorm(h_lat, gains[0]), sh1, sc1)
        u_ctx = modulate(rmsnorm(h_ctx, gains[0]), sh1c, sc1c)
        if col_major:
            u_lat = grid_transpose(u_lat, rows, GRID_W)
        p_lat = jnp.split(u_lat @ w_in[layer], offs, axis=-1)
        p_ctx = jnp.split(u_ctx @ w_in[layer], offs, axis=-1)
        hy_args = (hy_short_w[layer], hy_short_b[layer], hy_ff_w1[layer], hy_ff_b1[layer], hy_ff_w2[layer],
                   hy_ff_b2[layer], hy_ff_w3[layer], hy_freq[layer], hy_skip[layer])
        lru_args = (lru_conv_w[layer], lru_conv_b[layer], lru_wr[layer], lru_br[layer], lru_wi[layer],
                    lru_bi[layer], lru_lambda[layer])
        proj_args = (w_proj_hy[layer], w_proj_lru[layer], w_proj_hg[layer], w_out[layer])
        y_hy_lat = hyena_mixer(p_lat[0], *hy_args)
        y_lru_ctx, y_lru_lat = rglru_mixer(p_ctx[1], p_ctx[2], p_lat[1], p_lat[2], *lru_args, need_ctx)
        y_hg_ctx, y_hg_lat = hgrn2_mixer(p_ctx[3:8], p_lat[3:8], lb_all[layer], hg_norm_g[layer], need_ctx)
        m_lat = merge_branches(p_lat[8], y_hy_lat, y_lru_lat, y_hg_lat, *proj_args)
        if col_major:
            m_lat = grid_transpose(m_lat, GRID_W, rows)
        h_lat = h_lat + g1 * rmsnorm(m_lat, gains[1])
        if need_ctx:
            y_hy_ctx = hyena_mixer(p_ctx[0], *hy_args)
            m_ctx = merge_branches(p_ctx[8], y_hy_ctx, y_lru_ctx, y_hg_ctx, *proj_args)
            h_ctx = h_ctx + g1c * rmsnorm(m_ctx, gains[1])

        f_lat = sq_relu_mlp(modulate(rmsnorm(h_lat, gains[2]), sh2, sc2), w_mlp1[layer], w_mlp2[layer])
        h_lat = h_lat + g2 * rmsnorm(f_lat, gains[3])
        if need_ctx:
            f_ctx = sq_relu_mlp(modulate(rmsnorm(h_ctx, gains[2]), sh2c, sc2c), w_mlp1[layer], w_mlp2[layer])
            h_ctx = h_ctx + g2c * rmsnorm(f_ctx, gains[3])
    return h_lat
```

```python
import functools
import math

import jax
import jax.numpy as jnp
from jax import lax
from jax.experimental import pallas as pl
from jax.experimental.pallas import tpu as pltpu

F32 = jnp.float32
BF16 = jnp.bfloat16
HIGHEST = lax.Precision.HIGHEST

D_MODEL = 1024
DEPTH = 2
GRID_W = 64
EPS = 1e-6

E_HY = 512
HY_EMB_BANDS = 8
HY_ORDER = 64
HY_DECAY_TARGET = 1e-2
HY_FAST_DECAY = 0.3
HY_SLOW_DECAY = 1.5

E_LRU = 512
LRU_BLOCKS = 8
LRU_BLOCK_DIM = E_LRU // LRU_BLOCKS
LRU_CONV = 4
LRU_C = 8.0

HG_HEADS = 4
HG_DK = 128
HG_DV = 128
E_HG = HG_HEADS * HG_DV
HG_CHUNK = 64

D_FF = 4 * D_MODEL
IN_WIDTH = 8 * D_MODEL
N_ADA = 6

COL_HY_V, COL_HY_X0, COL_HY_X1 = 0, 1, 2
COL_LRU_X, COL_LRU_G = 3, 4
COL_HG_Q, COL_HG_FF, COL_HG_FB, COL_HG_I, COL_HG_OG = 5, 6, 7, 8, 9
COL_GATES = 5

LANES = 128
SUBLANES = 8
VMEM_LIMIT = 56 * 1024 * 1024


def _cparams(*sem):
    return pltpu.CompilerParams(dimension_semantics=sem, vmem_limit_bytes=VMEM_LIMIT)


def _rms(x):
    return x * lax.rsqrt(jnp.mean(x * x, axis=-1, keepdims=True) + EPS)


def _silu(x):
    return x * jax.nn.sigmoid(x)


def _gelu_tanh(x):
    return x * (0.5 * (1.0 + jnp.tanh(math.sqrt(2.0 / math.pi) * (x + 0.044715 * (x * x * x)))))


def _dot(a, b):
    return jnp.dot(a, b, preferred_element_type=F32)


def _dot_nt(a, b):
    return lax.dot_general(a, b, (((1,), (1,)), ((), ())), preferred_element_type=F32)


def _dot_tn(a, b):
    return lax.dot_general(a, b, (((0,), (0,)), ((), ())), preferred_element_type=F32)


def _ada_spec(layer, row_fn, chunk, nargs):
    if nargs == 2:
        return pl.BlockSpec((None, None, 1, D_MODEL), lambda b, i: (layer, row_fn(b), 0, chunk))
    return pl.BlockSpec((None, None, 1, D_MODEL), lambda b, i, j: (layer, row_fn(b), 0, chunk))


def _gain_spec(layer, idx, nargs):
    if nargs == 2:
        return pl.BlockSpec((None, None, 1, D_MODEL), lambda b, i: (layer, idx, 0, 0))
    return pl.BlockSpec((None, None, 1, D_MODEL), lambda b, i, j: (layer, idx, 0, 0))


def _ada_kernel(c_ref, w_ref, b_ref, o_ref):
    s = _silu(c_ref[...])
    o_ref[...] = _dot(s.astype(BF16), w_ref[...]) + b_ref[...]


def _ada_call(cin, w_ada_b, b_ada):
    tn = D_MODEL
    rows = cin.shape[0]
    return pl.pallas_call(
        _ada_kernel,
        out_shape=jax.ShapeDtypeStruct((DEPTH, rows, N_ADA * D_MODEL), F32),
        grid=(DEPTH, N_ADA * D_MODEL // tn),
        in_specs=[
            pl.BlockSpec((rows, D_MODEL), lambda l, j: (0, 0)),
            pl.BlockSpec((None, D_MODEL, tn), lambda l, j: (l, 0, j)),
            pl.BlockSpec((None, 1, tn), lambda l, j: (l, 0, j)),
        ],
        out_specs=pl.BlockSpec((None, rows, tn), lambda l, j: (l, 0, j)),
        compiler_params=_cparams("parallel", "parallel"),
        name="ada",
    )(cin, w_ada_b, b_ada.reshape(DEPTH, 1, N_ADA * D_MODEL))


def _inproj_kernel(h_ref, sh_ref, sc_ref, g_ref, w_ref, o_ref, u_ref, *, n_cols, n_rows):
    @pl.when(pl.program_id(2) == 0)
    def _():
        mul = g_ref[...] * (1.0 + sc_ref[...])
        sh = sh_ref[...]
        if n_cols:
            for c in range(n_cols):
                x = h_ref[:, c * D_MODEL:(c + 1) * D_MODEL]
                u_ref[c * n_rows:(c + 1) * n_rows, :] = (_rms(x) * mul + sh).astype(BF16)
        else:
            u_ref[...] = (_rms(h_ref[...]) * mul + sh).astype(BF16)

    o_ref[...] = _dot(u_ref[...], w_ref[...]).astype(BF16)


def _in_proj(h, ada4, gains4, w_in_b, layer, *, ctx_row, col_major):
    nb, t, _ = h.shape
    tm, tn = min(1024, t), 1024
    row_fn = (lambda b: b) if ctx_row is None else (lambda b: ctx_row)
    if col_major:
        n_rows = t // GRID_W
        n_cols = tm // n_rows
        h = h.reshape(nb, n_rows, GRID_W * D_MODEL)
        h_spec = pl.BlockSpec((None, n_rows, n_cols * D_MODEL), lambda b, i, j: (b, 0, i))
    else:
        n_rows = n_cols = 0
        h_spec = pl.BlockSpec((None, tm, D_MODEL), lambda b, i, j: (b, i, 0))
    return pl.pallas_call(
        functools.partial(_inproj_kernel, n_cols=n_cols, n_rows=n_rows),
        out_shape=jax.ShapeDtypeStruct((nb, t, IN_WIDTH), BF16),
        grid=(nb, t // tm, IN_WIDTH // tn),
        in_specs=[
            h_spec,
            _ada_spec(layer, row_fn, 0, 3),
            _ada_spec(layer, row_fn, 1, 3),
            _gain_spec(layer, 0, 3),
            pl.BlockSpec((None, D_MODEL, tn), lambda b, i, j: (layer, 0, j)),
        ],
        out_specs=pl.BlockSpec((None, tm, tn), lambda b, i, j: (b, i, j)),
        scratch_shapes=[pltpu.VMEM((tm, D_MODEL), BF16)],
        compiler_params=_cparams("parallel", "parallel", "arbitrary"),
        name="in_proj",
    )(h, ada4, ada4, gains4, w_in_b)


def _dft_tables(length):
    n = 2 * length
    idx = jnp.arange(length, dtype=jnp.int32)
    ang = ((idx[:, None] * idx[None, :]) % n).astype(F32) * (2.0 * math.pi / n)
    sign = (1 - 2 * (idx % 2)).astype(F32)
    sign_rows = jnp.zeros((SUBLANES, length), F32).at[0].set(sign)
    return jnp.cos(ang).astype(BF16), jnp.sin(ang).astype(BF16), sign_rows.astype(BF16)


def _hy_taps_kernel(w1_ref, b1_ref, w2_ref, b2_ref, w3_ref, fr_ref, hs_ref, hd_ref, *, length, tl):
    row0 = pl.program_id(0) * tl
    pos = (lax.broadcasted_iota(jnp.int32, (tl, LANES), 0) + row0).astype(F32)
    lane = lax.broadcasted_iota(jnp.int32, (tl, LANES), 1)
    t = pos / max(length - 1, 1)
    band_step = (HY_EMB_BANDS - 1 - 1e-4) / (HY_EMB_BANDS - 1)
    band = 1e-4 + jnp.where(lane <= HY_EMB_BANDS, lane - 1, lane - 1 - HY_EMB_BANDS).astype(F32) * band_step
    ang = band * (2.0 * math.pi * pos / length)
    z = jnp.where(lane == 0, t,
                  jnp.where(lane <= HY_EMB_BANDS, jnp.cos(ang),
                            jnp.where(lane <= 2 * HY_EMB_BANDS, -jnp.sin(ang), 0.0)))
    h = jnp.sin(fr_ref[0:1, :] * (jnp.dot(z, w1_ref[...], precision=HIGHEST, preferred_element_type=F32) + b1_ref[...]))
    h = jnp.sin(fr_ref[1:2, :] * (jnp.dot(h, w2_ref[...], precision=HIGHEST, preferred_element_type=F32) + b2_ref[...]))
    h = jnp.dot(h, w3_ref[...], precision=HIGHEST, preferred_element_type=F32)
    ch = lax.broadcasted_iota(jnp.int32, (1, E_HY), 1).astype(F32)
    lo = math.log(HY_DECAY_TARGET) / HY_SLOW_DECAY
    hi = math.log(HY_DECAY_TARGET) / HY_FAST_DECAY
    delta = jnp.abs(lo + ch * ((hi - lo) / (E_HY - 1)))
    rows = lax.broadcasted_iota(jnp.int32, (tl, E_HY), 0) + row0
    decay = jnp.exp(-(rows.astype(F32) / max(length - 1, 1)) * delta)
    h_fwd = h[:, :E_HY] * decay
    h_bwd = jnp.where(rows == 0, 0.0, h[:, E_HY:] * decay)
    hs_ref[...] = (h_fwd + h_bwd).astype(BF16)
    hd_ref[...] = (h_bwd - h_fwd).astype(BF16)


def _hy_taps(length, layer, w1p, b1, w2, b2, w3, freq):
    tl = 256
    full = lambda shape: pl.BlockSpec((None,) + shape, lambda i: (layer,) + (0,) * len(shape))
    return pl.pallas_call(
        functools.partial(_hy_taps_kernel, length=length, tl=tl),
        out_shape=[jax.ShapeDtypeStruct((length, E_HY), BF16)] * 2,
        grid=(length // tl,),
        in_specs=[full((LANES, HY_ORDER)), full((1, HY_ORDER)), full((HY_ORDER, HY_ORDER)), full((1, HY_ORDER)),
                  full((HY_ORDER, 2 * E_HY)), full((2, HY_ORDER))],
        out_specs=[pl.BlockSpec((tl, E_HY), lambda i: (i, 0))] * 2,
        compiler_params=_cparams("parallel"),
        name="hy_taps",
    )(w1p, b1, w2, b2, w3, freq)


def _hy_spec_kernel(c_ref, s_ref, sg_ref, hs_ref, hd_ref, kr_ref, ki_ref, kn_ref, *, length, tf):
    n = 2 * length
    row = lax.broadcasted_iota(jnp.int32, (tf, E_HY), 0) + pl.program_id(0) * tf
    scale = jnp.where(row == 0, 1.0 / n, 2.0 / n)
    kr_ref[...] = _dot(c_ref[...], hs_ref[...]) * scale
    ki_ref[...] = _dot(s_ref[...], hd_ref[...]) * scale
    kn_ref[...] = _dot(sg_ref[...], hs_ref[...]) * (1.0 / n)


def _hy_spectrum(length, cos_t, sin_t, sign_rows, hs, hd):
    tf = 256
    return pl.pallas_call(
        functools.partial(_hy_spec_kernel, length=length, tf=tf),
        out_shape=[jax.ShapeDtypeStruct((length, E_HY), F32), jax.ShapeDtypeStruct((length, E_HY), F32),
                   jax.ShapeDtypeStruct((SUBLANES, E_HY), F32)],
        grid=(length // tf,),
        in_specs=[
            pl.BlockSpec((tf, length), lambda i: (i, 0)),
            pl.BlockSpec((tf, length), lambda i: (i, 0)),
            pl.BlockSpec((SUBLANES, length), lambda i: (0, 0)),
            pl.BlockSpec((length, E_HY), lambda i: (0, 0)),
            pl.BlockSpec((length, E_HY), lambda i: (0, 0)),
        ],
        out_specs=[pl.BlockSpec((tf, E_HY), lambda i: (i, 0)), pl.BlockSpec((tf, E_HY), lambda i: (i, 0)),
                   pl.BlockSpec((SUBLANES, E_HY), lambda i: (0, 0))],
        compiler_params=_cparams("arbitrary"),
        name="hy_spectrum",
    )(cos_t, sin_t, sign_rows, hs, hd)


def _hyena_kernel(v_ref, x0_ref, x1_ref, sw_ref, sb_ref, skip_ref, sg_ref, kn_ref,
                  cr_ref, sr_ref, cc_ref, sc_ref, kr_ref, ki_ref, o_ref, u_ref, x0s_ref, acc_ref, *, length):
    f = pl.program_id(1)

    @pl.when(f == 0)
    def _():
        row = lax.broadcasted_iota(jnp.int32, (length, E_HY), 0)

        def short_conv(ref, j):
            x = ref[...].astype(F32)
            w = sw_ref[:, j * E_HY:(j + 1) * E_HY]
            prev = jnp.where(row == 0, 0.0, pltpu.roll(x, 1, 0))
            nxt = jnp.where(row == length - 1, 0.0, pltpu.roll(x, length - 1, 0))
            return w[0:1] * prev + w[1:2] * x + w[2:3] * nxt + sb_ref[:, j * E_HY:(j + 1) * E_HY]

        u = short_conv(v_ref, 0) * short_conv(x1_ref, 2)
        x0s_ref[...] = short_conv(x0_ref, 1).astype(BF16)
        ub = u.astype(BF16)
        u_ref[...] = ub
        nyq = _dot(sg_ref[...], ub)[0:1] * kn_ref[0:1, :]
        sign = (1 - 2 * (row & 1)).astype(F32)
        acc_ref[...] = u * skip_ref[...] + sign * nyq

    ub = u_ref[...]
    ur = _dot(cr_ref[...], ub)
    ui = _dot(sr_ref[...], ub)
    kr = kr_ref[...]
    ki = ki_ref[...]
    yr = (ur * kr + ui * ki).astype(BF16)
    yi = (ui * kr - ur * ki).astype(BF16)
    acc_ref[...] += _dot(cc_ref[...], yr) + _dot(sc_ref[...], yi)

    @pl.when(f == pl.num_programs(1) - 1)
    def _():
        o_ref[...] = (x0s_ref[...].astype(F32) * acc_ref[...]).astype(BF16)


def _hyena(p, layer, short_w, short_b, skip, tables, spectrum):
    nb, length, _ = p.shape
    cos_t, sin_t, sign_rows = tables
    kr, ki, kn = spectrum
    tf = 256
    col = lambda j: pl.BlockSpec((None, length, E_HY), lambda b, f: (b, 0, j))
    return pl.pallas_call(
        functools.partial(_hyena_kernel, length=length),
        out_shape=jax.ShapeDtypeStruct((nb, length, E_HY), BF16),
        grid=(nb, length // tf),
        in_specs=[
            col(COL_HY_V), col(COL_HY_X0), col(COL_HY_X1),
            pl.BlockSpec((None, 3, 3 * E_HY), lambda b, f: (layer, 0, 0)),
            pl.BlockSpec((None, 1, 3 * E_HY), lambda b, f: (layer, 0, 0)),
            pl.BlockSpec((None, 1, E_HY), lambda b, f: (layer, 0, 0)),
            pl.BlockSpec((SUBLANES, length), lambda b, f: (0, 0)),
            pl.BlockSpec((SUBLANES, E_HY), lambda b, f: (0, 0)),
            pl.BlockSpec((tf, length), lambda b, f: (f, 0)),
            pl.BlockSpec((tf, length), lambda b, f: (f, 0)),
            pl.BlockSpec((length, tf), lambda b, f: (0, f)),
            pl.BlockSpec((length, tf), lambda b, f: (0, f)),
            pl.BlockSpec((tf, E_HY), lambda b, f: (f, 0)),
            pl.BlockSpec((tf, E_HY), lambda b, f: (f, 0)),
        ],
        out_specs=pl.BlockSpec((None, length, E_HY), lambda b, f: (b, 0, 0)),
        scratch_shapes=[pltpu.VMEM((length, E_HY), BF16), pltpu.VMEM((length, E_HY), BF16),
                        pltpu.VMEM((length, E_HY), F32)],
        compiler_params=_cparams("parallel", "arbitrary"),
        name="hyena",
    )(p, p, p, short_w, short_b, skip, sign_rows, kn, cos_t, sin_t, cos_t, sin_t, kr, ki)


LRU_PAD = SUBLANES
LRU_CHUNK = 128


def _lru_kernel(xc_ref, gc_ref, xl_ref, gl_ref, cw_ref, cb_ref, wg_ref, bg_ref, lam_ref, yc_ref, yl_ref,
                xpad, a_cum, b_cum, h_sum, *, tc, tl):
    off_c = LRU_PAD
    off_l = off_c + tc + LRU_PAD
    n = LRU_CHUNK
    n_chunks_c = tc // n
    n_chunks = (tc + tl) // n
    groups_c = tc // SUBLANES
    groups = (tc + tl) // SUBLANES

    xpad[...] = jnp.zeros_like(xpad)
    xpad[off_c:off_c + tc, :] = xc_ref[...].astype(F32)
    xpad[off_l:off_l + tl, :] = xl_ref[...].astype(F32)

    sub = lax.broadcasted_iota(jnp.int32, (n // SUBLANES, SUBLANES, LANES), 1)

    for d in range(2):
        lam = lam_ref[d:d + 1, :]
        softplus_neg = jnp.maximum(-lam, 0.0) + jnp.log1p(jnp.exp(-jnp.abs(lam)))

        def phase_a(k, carry, d=d, softplus_neg=softplus_neg):
            start = pl.multiple_of(off_c + k * n + jnp.where(k >= n_chunks_c, LRU_PAD, 0), SUBLANES)
            for j in range(E_LRU // LANES):
                cols = slice(j * LANES, (j + 1) * LANES)
                win = xpad[pl.ds(start - LRU_PAD, n + 2 * LRU_PAD), cols]
                xc = cb_ref[d:d + 1, cols] + jnp.zeros((n, LANES), F32)
                for tap in range(LRU_CONV):
                    back = LRU_CONV - 1 - tap
                    shift = back if d == 0 else (n + 2 * LRU_PAD - back) % (n + 2 * LRU_PAD)
                    src = win if back == 0 else pltpu.roll(win, shift, 0)
                    xc = xc + cw_ref[d, tap:tap + 1, cols] * src[LRU_PAD:LRU_PAD + n]
                gates = _dot(xc.astype(BF16), wg_ref[d, j]) + bg_ref[d, j:j + 1, :]
                r = jax.nn.sigmoid(gates[:, :LANES])
                i = jax.nn.sigmoid(gates[:, LANES:])
                log_a = (-LRU_C) * r * softplus_neg[:, cols]
                a = jnp.exp(log_a)
                b = jnp.sqrt(-jnp.tanh(log_a) * (a * a + 1.0)) * (i * xc)
                a = a.reshape(n // SUBLANES, SUBLANES, LANES)
                b = b.reshape(n // SUBLANES, SUBLANES, LANES)
                for s in (1, 2, 4):
                    if d == 0:
                        keep = sub >= s
                        a_prev = pltpu.roll(a, s, 1)
                        b_prev = pltpu.roll(b, s, 1)
                    else:
                        keep = sub < SUBLANES - s
                        a_prev = pltpu.roll(a, SUBLANES - s, 1)
                        b_prev = pltpu.roll(b, SUBLANES - s, 1)
                    b = jnp.where(keep, a * b_prev + b, b)
                    a = jnp.where(keep, a * a_prev, a)
                a_cum[pl.ds(start, n), cols] = a.reshape(n, LANES)
                b_cum[pl.ds(start, n), cols] = b.reshape(n, LANES)
            return carry

        lax.fori_loop(0, n_chunks, phase_a, 0)

        def phase_b(g, h_prev, d=d):
            if d == 0:
                row = off_c + g * SUBLANES + jnp.where(g >= groups_c, LRU_PAD, 0)
            else:
                row = jnp.where(g < groups_c, off_c + (groups_c - 1 - g) * SUBLANES,
                                off_l + (groups - 1 - g) * SUBLANES)
            rows = pl.ds(pl.multiple_of(row, SUBLANES), SUBLANES)
            h = a_cum[rows, :] * h_prev + b_cum[rows, :]
            if d == 0:
                h_sum[rows, :] = h
                return jnp.broadcast_to(h[SUBLANES - 1:SUBLANES, :], (SUBLANES, E_LRU))
            h_sum[rows, :] = h_sum[rows, :] + h
            return jnp.broadcast_to(h[0:1, :], (SUBLANES, E_LRU))

        lax.fori_loop(0, groups, phase_b, jnp.zeros((SUBLANES, E_LRU), F32), unroll=4)

    yc_ref[...] = (h_sum[off_c:off_c + tc, :] * _gelu_tanh(gc_ref[...].astype(F32))).astype(BF16)

    def phase_c(k, carry):
        r0 = pl.multiple_of(k * 256, 256)
        yl_ref[pl.ds(r0, 256), :] = (h_sum[pl.ds(off_l + r0, 256), :]
                                     * _gelu_tanh(gl_ref[pl.ds(r0, 256), :].astype(F32))).astype(BF16)
        return carry

    lax.fori_loop(0, tl // 256, phase_c, 0)


def _lru_gate_weights(wr, wi, br, bi):
    per = LANES // LRU_BLOCK_DIM
    nj = E_LRU // LANES

    def lane_blocks(w):
        w = w.reshape(DEPTH, 2, nj, per, LRU_BLOCK_DIM, LRU_BLOCK_DIM)
        eye = jnp.eye(per, dtype=w.dtype)
        full = w[:, :, :, :, :, None, :] * eye[None, None, None, :, None, :, None]
        return full.reshape(DEPTH, 2, nj, LANES, LANES)

    wg = jnp.concatenate([lane_blocks(wr), lane_blocks(wi)], axis=-1).astype(BF16)
    bg = jnp.concatenate([br.reshape(DEPTH, 2, nj, LANES), bi.reshape(DEPTH, 2, nj, LANES)], axis=-1)
    return wg, bg


def _lru(p_ctx, p_lat, layer, conv_w, conv_b, wg, bg, lam):
    nb, tc, _ = p_ctx.shape
    tl = p_lat.shape[1]
    rows = tc + tl + 3 * LRU_PAD
    col = lambda t, j: pl.BlockSpec((None, t, E_LRU), lambda b: (b, 0, j))
    par = lambda shape: pl.BlockSpec((None,) + shape, lambda b: (layer,) + (0,) * len(shape))
    return pl.pallas_call(
        functools.partial(_lru_kernel, tc=tc, tl=tl),
        out_shape=[jax.ShapeDtypeStruct((nb, tc, E_LRU), BF16), jax.ShapeDtypeStruct((nb, tl, E_LRU), BF16)],
        grid=(nb,),
        in_specs=[col(tc, COL_LRU_X), col(tc, COL_LRU_G), col(tl, COL_LRU_X), col(tl, COL_LRU_G),
                  par((2, LRU_CONV, E_LRU)), par((2, E_LRU)), par((2, E_LRU // LANES, LANES, 2 * LANES)),
                  par((2, E_LRU // LANES, 2 * LANES)), par((2, E_LRU))],
        out_specs=[pl.BlockSpec((None, tc, E_LRU), lambda b: (b, 0, 0)),
                   pl.BlockSpec((None, tl, E_LRU), lambda b: (b, 0, 0))],
        scratch_shapes=[pltpu.VMEM((rows, E_LRU), F32)] * 4,
        compiler_params=_cparams("parallel"),
        name="lru",
    )(p_ctx, p_ctx, p_lat, p_lat, conv_w, conv_b, wg, bg, lam)


def _hgrn_kernel(qc_ref, ffc_ref, fbc_ref, ic_ref, ogc_ref, ql_ref, ffl_ref, fbl_ref, il_ref, ogl_ref,
                 lb_ref, ng_ref, yc_ref, yl_ref, o_acc, state, *, layer, need_ctx, tc, tl):
    ch = HG_CHUNK
    raw = [lb_ref[i:i + 1, :] for i in range(DEPTH)]
    top = functools.reduce(jnp.maximum, raw)
    ex = [jnp.exp(r - top) for r in raw]
    tot = functools.reduce(lambda a, b: a + b, ex)
    lb = jnp.zeros((1, E_HG), F32)
    for i in range(1, layer + 1):
        lb = lb + ex[i] / tot

    state[...] = jnp.zeros_like(state)
    o_acc[...] = jnp.zeros_like(o_acc)

    rt = lax.broadcasted_iota(jnp.int32, (ch, ch), 0)
    ct = lax.broadcasted_iota(jnp.int32, (ch, ch), 1)
    tri = [(rt >= ct), (rt <= ct)]
    tri_f = [t.astype(F32) for t in tri]

    def run(q_ref, ff_ref, fb_ref, i_ref, n_chunks, base, need_out):
        def body(it, carry):
            for d in range(2):
                c = it if d == 0 else n_chunks - 1 - it
                rows = pl.ds(pl.multiple_of(c * ch, ch), ch)
                f_ref = ff_ref if d == 0 else fb_ref
                for hd in range(HG_HEADS):
                    cols = slice(hd * HG_DK, (hd + 1) * HG_DK)
                    lbh = lb[:, cols]
                    f = lbh + (1.0 - lbh) * jax.nn.sigmoid(f_ref[rows, cols].astype(F32))
                    k = 1.0 - f
                    v = i_ref[rows, cols]
                    b = jnp.dot(tri_f[d], jnp.log(f), precision=HIGHEST, preferred_element_type=F32)
                    if d == 0:
                        b_last, b_mid = b[ch - 1:ch], b[ch // 2:ch // 2 + 1]
                    else:
                        b_last, b_mid = b[0:1], b[ch // 2 - 1:ch // 2]
                    k_dec = (k * jnp.exp(b_last - b)).astype(BF16)
                    s_t = state[d * HG_HEADS + hd]
                    if need_out:
                        q = _silu(q_ref[rows, cols].astype(F32))
                        qs = (q * jnp.exp(b - b_mid)).astype(BF16)
                        ks = (k * jnp.exp(b_mid - b)).astype(BF16)
                        qg = (q * jnp.exp(b)).astype(BF16)
                        scores = jnp.where(tri[d], _dot_nt(qs, ks), 0.0)
                        o = _dot(scores.astype(BF16), v) + _dot_nt(qg, s_t.astype(BF16))
                        orow = pl.ds(pl.multiple_of(base + c * ch, ch), ch)
                        o_acc[orow, cols] = o_acc[orow, cols] + o
                    state[d * HG_HEADS + hd] = s_t * jnp.exp(b_last) + _dot_tn(v, k_dec)
            return carry

        lax.fori_loop(0, n_chunks, body, 0)

    run(qc_ref, ffc_ref, fbc_ref, ic_ref, tc // ch, 0, need_ctx)
    run(ql_ref, ffl_ref, fbl_ref, il_ref, tl // ch, tc, True)

    def finish(og_ref, y_ref, base, t):
        blk = min(t, 256)

        def body(k, carry):
            r0 = pl.multiple_of(k * blk, blk)
            for hd in range(HG_HEADS):
                cols = slice(hd * HG_DV, (hd + 1) * HG_DV)
                o = o_acc[pl.ds(base + r0, blk), cols]
                y = _rms(o) * ng_ref[...] * _silu(og_ref[pl.ds(r0, blk), cols].astype(F32))
                y_ref[pl.ds(r0, blk), cols] = y.astype(BF16)
            return carry

        lax.fori_loop(0, t // blk, body, 0)

    finish(ogc_ref, yc_ref, 0, tc)
    finish(ogl_ref, yl_ref, tc, tl)


def _hgrn(p_ctx, p_lat, layer, lower_bounds, norm_g, need_ctx):
    nb, tc, _ = p_ctx.shape
    tl = p_lat.shape[1]
    col = lambda t, j: pl.BlockSpec((None, t, E_HG), lambda b: (b, 0, j))
    names = (COL_HG_Q, COL_HG_FF, COL_HG_FB, COL_HG_I, COL_HG_OG)
    return pl.pallas_call(
        functools.partial(_hgrn_kernel, layer=layer, need_ctx=need_ctx, tc=tc, tl=tl),
        out_shape=[jax.ShapeDtypeStruct((nb, tc, E_HG), BF16), jax.ShapeDtypeStruct((nb, tl, E_HG), BF16)],
        grid=(nb,),
        in_specs=[col(tc, j) for j in names] + [col(tl, j) for j in names] + [
            pl.BlockSpec((DEPTH, E_HG), lambda b: (0, 0)),
            pl.BlockSpec((None, 1, HG_DV), lambda b: (layer, 0, 0)),
        ],
        out_specs=[pl.BlockSpec((None, tc, E_HG), lambda b: (b, 0, 0)),
                   pl.BlockSpec((None, tl, E_HG), lambda b: (b, 0, 0))],
        scratch_shapes=[pltpu.VMEM((tc + tl, E_HG), F32), pltpu.VMEM((2 * HG_HEADS, HG_DV, HG_DK), F32)],
        compiler_params=_cparams("parallel"),
        name="hgrn",
    )(*([p_ctx] * 5 + [p_lat] * 5 + [lower_bounds, norm_g]))


def _merge_kernel(gh_ref, gl_ref, gg_ref, yh_ref, yl_ref, yg_ref, wh_ref, wl_ref, wg_ref, wo_ref,
                  h_ref, g1_ref, gain_ref, o_ref, *, n_cols, n_rows):
    def branch(g_ref, y_ref, w_ref):
        return jax.nn.sigmoid(g_ref[...].astype(F32)) * _dot(y_ref[...], w_ref[...])

    m = branch(gh_ref, yh_ref, wh_ref) + branch(gl_ref, yl_ref, wl_ref) + branch(gg_ref, yg_ref, wg_ref)
    r = _rms(_dot(m.astype(BF16), wo_ref[...])) * (gain_ref[...] * g1_ref[...])
    if n_cols:
        for c in range(n_cols):
            lanes = slice(c * D_MODEL, (c + 1) * D_MODEL)
            o_ref[:, lanes] = h_ref[:, lanes] + r[c * n_rows:(c + 1) * n_rows]
    else:
        o_ref[...] = h_ref[...] + r


def _merge(p, y_hy, y_lru, y_hg, w_hy, w_lru, w_hg, w_out, h, ada4, gains4, layer, *, ctx_row, col_major):
    nb, t, _ = h.shape
    tm = 512
    row_fn = (lambda b: b) if ctx_row is None else (lambda b: ctx_row)
    if col_major:
        n_rows = t // GRID_W
        n_cols = tm // n_rows
        h = h.reshape(nb, n_rows, GRID_W * D_MODEL)
        h_spec = pl.BlockSpec((None, n_rows, n_cols * D_MODEL), lambda b, i: (b, 0, i))
    else:
        n_rows = n_cols = 0
        h_spec = pl.BlockSpec((None, tm, D_MODEL), lambda b, i: (b, i, 0))
    gate = lambda k: pl.BlockSpec((None, tm, D_MODEL), lambda b, i: (b, i, COL_GATES + k))
    y_spec = pl.BlockSpec((None, tm, E_HY), lambda b, i: (b, i, 0))
    w_spec = pl.BlockSpec((None, E_HY, D_MODEL), lambda b, i: (layer, 0, 0))
    out = pl.pallas_call(
        functools.partial(_merge_kernel, n_cols=n_cols, n_rows=n_rows),
        out_shape=jax.ShapeDtypeStruct(h.shape, F32),
        grid=(nb, t // tm),
        in_specs=[gate(0), gate(1), gate(2), y_spec, y_spec, y_spec, w_spec, w_spec, w_spec,
                  pl.BlockSpec((None, D_MODEL, D_MODEL), lambda b, i: (layer, 0, 0)),
                  h_spec, _ada_spec(layer, row_fn, 2, 2), _gain_spec(layer, 1, 2)],
        out_specs=h_spec,
        compiler_params=_cparams("parallel", "parallel"),
        name="merge",
    )(p, p, p, y_hy, y_lru, y_hg, w_hy, w_lru, w_hg, w_out, h, ada4, gains4)
    return out.reshape(nb, t, D_MODEL)


def _mlp_kernel(h_ref, sh_ref, sc_ref, g2_ref, gain2_ref, gain3_ref, w1_ref, w2_ref, o_ref, u_ref, acc_ref):
    k = pl.program_id(2)

    @pl.when(k == 0)
    def _():
        u_ref[...] = (_rms(h_ref[...]) * (gain2_ref[...] * (1.0 + sc_ref[...])) + sh_ref[...]).astype(BF16)
        acc_ref[...] = jnp.zeros_like(acc_ref)

    a = jnp.maximum(_dot(u_ref[...], w1_ref[...]), 0.0)
    acc_ref[...] += _dot((a * a).astype(BF16), w2_ref[...])

    @pl.when(k == pl.num_programs(2) - 1)
    def _():
        o_ref[...] = h_ref[...] + (g2_ref[...] * gain3_ref[...]) * _rms(acc_ref[...])


def _mlp(h, ada4, gains4, w1_b, w2_b, layer, *, ctx_row):
    nb, t, _ = h.shape
    tm, tf = min(1024, t), 1024
    row_fn = (lambda b: b) if ctx_row is None else (lambda b: ctx_row)
    h_spec = pl.BlockSpec((None, tm, D_MODEL), lambda b, i, k: (b, i, 0))
    return pl.pallas_call(
        _mlp_kernel,
        out_shape=jax.ShapeDtypeStruct(h.shape, F32),
        grid=(nb, t // tm, D_FF // tf),
        in_specs=[h_spec, _ada_spec(layer, row_fn, 3, 3), _ada_spec(layer, row_fn, 4, 3),
                  _ada_spec(layer, row_fn, 5, 3), _gain_spec(layer, 2, 3), _gain_spec(layer, 3, 3),
                  pl.BlockSpec((None, D_MODEL, tf), lambda b, i, k: (layer, 0, k)),
                  pl.BlockSpec((None, tf, D_MODEL), lambda b, i, k: (layer, k, 0))],
        out_specs=h_spec,
        scratch_shapes=[pltpu.VMEM((tm, D_MODEL), BF16), pltpu.VMEM((tm, D_MODEL), F32)],
        compiler_params=_cparams("parallel", "parallel", "arbitrary"),
        name="mlp",
    )(h, ada4, ada4, ada4, gains4, gains4, w1_b, w2_b)


def kernel(x, c, ctx, c_ctx, w_ada, b_ada, norm_gains, w_in, hy_short_w, hy_short_b, hy_ff_w1, hy_ff_b1,
           hy_ff_w2, hy_ff_b2, hy_ff_w3, hy_freq, hy_skip, lru_conv_w, lru_conv_b, lru_wr, lru_br, lru_wi,
           lru_bi, lru_lambda, hg_lower_bounds, hg_norm_g, w_proj_hy, w_proj_lru, w_proj_hg, w_out,
           w_mlp1, w_mlp2):
    nb, seq, _ = x.shape
    ctx_len = ctx.shape[1]
    assert seq % GRID_W == 0

    w_in_b, w_out_b = w_in.astype(BF16), w_out.astype(BF16)
    w_hy_b, w_lru_b, w_hg_b = w_proj_hy.astype(BF16), w_proj_lru.astype(BF16), w_proj_hg.astype(BF16)
    w1_b, w2_b = w_mlp1.astype(BF16), w_mlp2.astype(BF16)

    ctx_row = nb
    ada_rows = -(-(nb + 1) // SUBLANES) * SUBLANES
    cin = jnp.concatenate([c, c_ctx[None], jnp.zeros((ada_rows - nb - 1, D_MODEL), F32)], axis=0)
    ada4 = _ada_call(cin, w_ada.astype(BF16), b_ada).reshape(DEPTH, ada_rows, 1, N_ADA * D_MODEL)
    gains4 = norm_gains.reshape(DEPTH, 4, 1, D_MODEL)

    hy_w1p = jnp.pad(hy_ff_w1, ((0, 0), (0, LANES - hy_ff_w1.shape[1]), (0, 0)))
    hy_b1 = hy_ff_b1.reshape(DEPTH, 1, HY_ORDER)
    hy_b2 = hy_ff_b2.reshape(DEPTH, 1, HY_ORDER)
    hy_sb = hy_short_b.reshape(DEPTH, 1, 3 * E_HY)
    hy_sk = hy_skip.reshape(DEPTH, 1, E_HY)
    lru_wg, lru_bg = _lru_gate_weights(lru_wr, lru_wi, lru_br, lru_bi)
    hg_ng = hg_norm_g.reshape(DEPTH, 1, HG_DV)

    tables = {seq: _dft_tables(seq), ctx_len: _dft_tables(ctx_len)}

    def hyena(p, layer):
        length = p.shape[1]
        hs, hd = _hy_taps(length, layer, hy_w1p, hy_b1, hy_ff_w2, hy_b2, hy_ff_w3, hy_freq)
        spectrum = _hy_spectrum(length, *tables[length], hs, hd)
        return _hyena(p, layer, hy_short_w, hy_sb, hy_sk, tables[length], spectrum)

    h_lat = x
    h_ctx = ctx.reshape(1, nb * ctx_len, D_MODEL)
    for layer in range(DEPTH):
        need_ctx = layer < DEPTH - 1
        col_major = layer % 2 == 1
        p_lat = _in_proj(h_lat, ada4, gains4, w_in_b, layer, ctx_row=None, col_major=col_major)
        p_ctx_flat = _in_proj(h_ctx, ada4, gains4, w_in_b, layer, ctx_row=ctx_row, col_major=False)
        p_ctx = p_ctx_flat.reshape(nb, ctx_len, IN_WIDTH)

        y_hy_lat = hyena(p_lat, layer)
        y_lru_ctx, y_lru_lat = _lru(p_ctx, p_lat, layer, lru_conv_w, lru_conv_b, lru_wg, lru_bg, lru_lambda)
        y_hg_ctx, y_hg_lat = _hgrn(p_ctx, p_lat, layer, hg_lower_bounds, hg_ng, need_ctx)
        proj = (w_hy_b, w_lru_b, w_hg_b, w_out_b)
        h_lat = _merge(p_lat, y_hy_lat, y_lru_lat, y_hg_lat, *proj, h_lat, ada4, gains4, layer,
                       ctx_row=None, col_major=col_major)
        if need_ctx:
            flat = lambda y: y.reshape(1, nb * ctx_len, y.shape[-1])
            y_hy_ctx = hyena(p_ctx, layer)
            h_ctx = _merge(p_ctx_flat, flat(y_hy_ctx), flat(y_lru_ctx), flat(y_hg_ctx), *proj, h_ctx, ada4,
                           gains4, layer, ctx_row=ctx_row, col_major=False)

        h_lat = _mlp(h_lat, ada4, gains4, w1_b, w2_b, layer, ctx_row=None)
        if need_ctx:
            h_ctx = _mlp(h_ctx, ada4, gains4, w1_b, w2_b, layer, ctx_row=ctx_row)
    return h_lat
```

```python
import functools
import math

import jax
import jax.numpy as jnp
from jax import lax
from jax.experimental import pallas as pl
from jax.experimental.pallas import tpu as pltpu

F32 = jnp.float32
BF16 = jnp.bfloat16
HIGHEST = lax.Precision.HIGHEST

D_MODEL = 1024
DEPTH = 2
GRID_W = 64
EPS = 1e-6

E_HY = 512
HY_EMB_BANDS = 8
HY_ORDER = 64
HY_DECAY_TARGET = 1e-2
HY_FAST_DECAY = 0.3
HY_SLOW_DECAY = 1.5
HY_ROWS = 64

E_LRU = 512
LRU_BLOCKS = 8
LRU_BLOCK_DIM = E_LRU // LRU_BLOCKS
LRU_CONV = 4
LRU_C = 8.0

HG_HEADS = 4
HG_DK = 128
HG_DV = 128
E_HG = HG_HEADS * HG_DV
HG_CHUNK = 64

D_FF = 4 * D_MODEL
IN_WIDTH = 8 * D_MODEL
N_ADA = 6

COL_HY_V, COL_HY_X0, COL_HY_X1 = 0, 1, 2
COL_LRU_X, COL_LRU_G = 3, 4
COL_HG_Q, COL_HG_FF, COL_HG_FB, COL_HG_I, COL_HG_OG = 5, 6, 7, 8, 9
COL_GATES = 5

LANES = 128
SUBLANES = 8
VMEM_LIMIT = 56 * 1024 * 1024
NORM_ROWS = 128


def _cparams(*sem):
    return pltpu.CompilerParams(dimension_semantics=sem, vmem_limit_bytes=VMEM_LIMIT)


def _rms(x):
    return x * lax.rsqrt(jnp.mean(x * x, axis=-1, keepdims=True) + EPS)


def _silu(x):
    return x * jax.nn.sigmoid(x)


def _gelu_tanh(x):
    return x * (0.5 * (1.0 + jnp.tanh(math.sqrt(2.0 / math.pi) * (x + 0.044715 * (x * x * x)))))


def _dot(a, b):
    return jnp.dot(a, b, preferred_element_type=F32)


def _dot_nt(a, b):
    return lax.dot_general(a, b, (((1,), (1,)), ((), ())), preferred_element_type=F32)


def _dot_tn(a, b):
    return lax.dot_general(a, b, (((0,), (0,)), ((), ())), preferred_element_type=F32)


def _ada_spec(layer, row_fn, chunk, nargs):
    if nargs == 2:
        return pl.BlockSpec((None, None, 1, D_MODEL), lambda b, i: (layer, row_fn(b), 0, chunk))
    return pl.BlockSpec((None, None, 1, D_MODEL), lambda b, i, j: (layer, row_fn(b), 0, chunk))


def _gain_spec(layer, idx, nargs):
    if nargs == 2:
        return pl.BlockSpec((None, None, 1, D_MODEL), lambda b, i: (layer, idx, 0, 0))
    return pl.BlockSpec((None, None, 1, D_MODEL), lambda b, i, j: (layer, idx, 0, 0))


def _ada_kernel(c_ref, w_ref, b_ref, o_ref):
    s = _silu(c_ref[...])
    o_ref[...] = _dot(s.astype(BF16), w_ref[...]) + b_ref[...]


def _ada_call(cin, w_ada_b, b_ada):
    tn = D_MODEL
    rows = cin.shape[0]
    return pl.pallas_call(
        _ada_kernel,
        out_shape=jax.ShapeDtypeStruct((DEPTH, rows, N_ADA * D_MODEL), F32),
        grid=(DEPTH, N_ADA * D_MODEL // tn),
        in_specs=[
            pl.BlockSpec((rows, D_MODEL), lambda l, j: (0, 0)),
            pl.BlockSpec((None, D_MODEL, tn), lambda l, j: (l, 0, j)),
            pl.BlockSpec((None, 1, tn), lambda l, j: (l, 0, j)),
        ],
        out_specs=pl.BlockSpec((None, rows, tn), lambda l, j: (l, 0, j)),
        compiler_params=_cparams("parallel", "parallel"),
        name="ada",
    )(cin, w_ada_b, b_ada.reshape(DEPTH, 1, N_ADA * D_MODEL))


def _inproj_kernel(h_ref, sh_ref, sc_ref, g_ref, w_ref, o_ref, u_ref, *t_ref, n_cols, n_rows):
    @pl.when(pl.program_id(2) == 0)
    def _():
        mul = g_ref[...] * (1.0 + sc_ref[...])
        sh = sh_ref[...]
        if n_cols:
            def grid_row(r, carry):
                y = _rms(h_ref[r]) * mul + sh
                rows = pl.ds(pl.multiple_of(r * n_cols, n_cols), n_cols)
                for j in range(D_MODEL // LANES):
                    t_ref[0][j, rows, :] = y[:, j * LANES:(j + 1) * LANES]
                return carry

            lax.fori_loop(0, n_rows, grid_row, 0)
            for j in range(D_MODEL // LANES):
                lanes = slice(j * LANES, (j + 1) * LANES)
                for c in range(n_cols):
                    u_ref[c * n_rows:(c + 1) * n_rows, lanes] = (
                        t_ref[0][j, pl.ds(c, n_rows, stride=n_cols), :].astype(BF16))
        else:
            def row_chunk(k, carry):
                rows = pl.ds(pl.multiple_of(k * NORM_ROWS, NORM_ROWS), NORM_ROWS)
                u_ref[rows, :] = (_rms(h_ref[rows, :]) * mul + sh).astype(BF16)
                return carry

            lax.fori_loop(0, u_ref.shape[0] // NORM_ROWS, row_chunk, 0)

    o_ref[...] = _dot(u_ref[...], w_ref[...]).astype(BF16)


def _in_proj(h, ada4, gains4, w_in_b, layer, *, ctx_row, col_major):
    nb, t, _ = h.shape
    tm, tn = min(2048, t), 1024
    row_fn = (lambda b: b) if ctx_row is None else (lambda b: ctx_row)
    if col_major:
        n_rows = t // GRID_W
        n_cols = tm // n_rows
        h = h.reshape(nb, n_rows, GRID_W, D_MODEL)
        h_spec = pl.BlockSpec((None, n_rows, n_cols, D_MODEL), lambda b, i, j: (b, 0, i, 0))
        scratch = [pltpu.VMEM((tm, D_MODEL), BF16), pltpu.VMEM((D_MODEL // LANES, tm, LANES), F32)]
    else:
        n_rows = n_cols = 0
        h_spec = pl.BlockSpec((None, tm, D_MODEL), lambda b, i, j: (b, i, 0))
        scratch = [pltpu.VMEM((tm, D_MODEL), BF16)]
    return pl.pallas_call(
        functools.partial(_inproj_kernel, n_cols=n_cols, n_rows=n_rows),
        out_shape=jax.ShapeDtypeStruct((nb, t, IN_WIDTH), BF16),
        grid=(nb, t // tm, IN_WIDTH // tn),
        in_specs=[
            h_spec,
            _ada_spec(layer, row_fn, 0, 3),
            _ada_spec(layer, row_fn, 1, 3),
            _gain_spec(layer, 0, 3),
            pl.BlockSpec((None, D_MODEL, tn), lambda b, i, j: (layer, 0, j)),
        ],
        out_specs=pl.BlockSpec((None, tm, tn), lambda b, i, j: (b, i, j)),
        scratch_shapes=scratch,
        compiler_params=_cparams("parallel", "parallel", "arbitrary"),
        name="in_proj",
    )(h, ada4, ada4, gains4, w_in_b)


def _dft_tables(length):
    n = 2 * length
    idx = jnp.arange(length, dtype=jnp.int32)
    ang = ((idx[:, None] * idx[None, :]) % n).astype(F32) * (2.0 * math.pi / n)
    sign = (1 - 2 * (idx % 2)).astype(F32)
    sign_rows = jnp.zeros((SUBLANES, length), F32).at[0].set(sign)
    return jnp.cos(ang).astype(BF16), jnp.sin(ang).astype(BF16), sign_rows.astype(BF16)


def _hy_taps_kernel(w1_ref, b1_ref, w2_ref, b2_ref, w3_ref, fr_ref, hs_ref, hd_ref, *, length, tl):
    row0 = pl.program_id(0) * tl
    pos = (lax.broadcasted_iota(jnp.int32, (tl, LANES), 0) + row0).astype(F32)
    lane = lax.broadcasted_iota(jnp.int32, (tl, LANES), 1)
    t = pos / max(length - 1, 1)
    band_step = (HY_EMB_BANDS - 1 - 1e-4) / (HY_EMB_BANDS - 1)
    band = 1e-4 + jnp.where(lane <= HY_EMB_BANDS, lane - 1, lane - 1 - HY_EMB_BANDS).astype(F32) * band_step
    ang = band * (2.0 * math.pi * pos / length)
    z = jnp.where(lane == 0, t,
                  jnp.where(lane <= HY_EMB_BANDS, jnp.cos(ang),
                            jnp.where(lane <= 2 * HY_EMB_BANDS, -jnp.sin(ang), 0.0)))
    h = jnp.sin(fr_ref[0:1, :] * (jnp.dot(z, w1_ref[...], precision=HIGHEST, preferred_element_type=F32) + b1_ref[...]))
    h = jnp.sin(fr_ref[1:2, :] * (jnp.dot(h, w2_ref[...], precision=HIGHEST, preferred_element_type=F32) + b2_ref[...]))
    h = jnp.dot(h, w3_ref[...], precision=HIGHEST, preferred_element_type=F32)
    ch = lax.broadcasted_iota(jnp.int32, (1, E_HY), 1).astype(F32)
    lo = math.log(HY_DECAY_TARGET) / HY_SLOW_DECAY
    hi = math.log(HY_DECAY_TARGET) / HY_FAST_DECAY
    delta = jnp.abs(lo + ch * ((hi - lo) / (E_HY - 1)))
    rows = lax.broadcasted_iota(jnp.int32, (tl, E_HY), 0) + row0
    decay = jnp.exp(-(rows.astype(F32) / max(length - 1, 1)) * delta)
    h_fwd = h[:, :E_HY] * decay
    h_bwd = jnp.where(rows == 0, 0.0, h[:, E_HY:] * decay)
    hs_ref[...] = (h_fwd + h_bwd).astype(BF16)
    hd_ref[...] = (h_bwd - h_fwd).astype(BF16)


def _hy_taps(length, layer, w1p, b1, w2, b2, w3, freq):
    tl = 256
    full = lambda shape: pl.BlockSpec((None,) + shape, lambda i: (layer,) + (0,) * len(shape))
    return pl.pallas_call(
        functools.partial(_hy_taps_kernel, length=length, tl=tl),
        out_shape=[jax.ShapeDtypeStruct((length, E_HY), BF16)] * 2,
        grid=(length // tl,),
        in_specs=[full((LANES, HY_ORDER)), full((1, HY_ORDER)), full((HY_ORDER, HY_ORDER)), full((1, HY_ORDER)),
                  full((HY_ORDER, 2 * E_HY)), full((2, HY_ORDER))],
        out_specs=[pl.BlockSpec((tl, E_HY), lambda i: (i, 0))] * 2,
        compiler_params=_cparams("parallel"),
        name="hy_taps",
    )(w1p, b1, w2, b2, w3, freq)


def _hy_spec_kernel(c_ref, s_ref, sg_ref, hs_ref, hd_ref, kr_ref, ki_ref, kn_ref, *, length, tf):
    n = 2 * length
    row = lax.broadcasted_iota(jnp.int32, (tf, E_HY), 0) + pl.program_id(0) * tf
    scale = jnp.where(row == 0, 1.0 / n, 2.0 / n)
    kr_ref[...] = _dot(c_ref[...], hs_ref[...]) * scale
    ki_ref[...] = _dot(s_ref[...], hd_ref[...]) * scale
    kn_ref[...] = _dot(sg_ref[...], hs_ref[...]) * (1.0 / n)


def _hy_spectrum(length, cos_t, sin_t, sign_rows, hs, hd):
    tf = 256
    return pl.pallas_call(
        functools.partial(_hy_spec_kernel, length=length, tf=tf),
        out_shape=[jax.ShapeDtypeStruct((length, E_HY), F32), jax.ShapeDtypeStruct((length, E_HY), F32),
                   jax.ShapeDtypeStruct((SUBLANES, E_HY), F32)],
        grid=(length // tf,),
        in_specs=[
            pl.BlockSpec((tf, length), lambda i: (i, 0)),
            pl.BlockSpec((tf, length), lambda i: (i, 0)),
            pl.BlockSpec((SUBLANES, length), lambda i: (0, 0)),
            pl.BlockSpec((length, E_HY), lambda i: (0, 0)),
            pl.BlockSpec((length, E_HY), lambda i: (0, 0)),
        ],
        out_specs=[pl.BlockSpec((tf, E_HY), lambda i: (i, 0)), pl.BlockSpec((tf, E_HY), lambda i: (i, 0)),
                   pl.BlockSpec((SUBLANES, E_HY), lambda i: (0, 0))],
        compiler_params=_cparams("arbitrary"),
        name="hy_spectrum",
    )(cos_t, sin_t, sign_rows, hs, hd)


def _hyena_kernel(v_ref, x0_ref, x1_ref, sw_ref, sb_ref, skip_ref, sg_ref, kn_ref,
                  cr_ref, sr_ref, cc_ref, sc_ref, kr_ref, ki_ref, o_ref, u_ref, x0s_ref, acc_ref, nyq_ref,
                  *, length):
    f = pl.program_id(1)

    n = HY_ROWS
    halo = 2 * SUBLANES
    row = lax.broadcasted_iota(jnp.int32, (n, E_HY), 0)

    @pl.when(f == 0)
    def _():
        def chunk(c, carry):
            r0 = pl.multiple_of(c * n, n)
            before = pl.ds(pl.multiple_of(jnp.maximum(r0 - halo, 0), halo), halo)
            after = pl.ds(pl.multiple_of(jnp.minimum(r0 + n, length - halo), halo), halo)

            def short_conv(ref, j):
                cols = slice(j * E_HY, (j + 1) * E_HY)
                x = ref[pl.ds(r0, n), :].astype(F32)
                x_before = jnp.where(c > 0, ref[before, :].astype(F32)[halo - 1:halo], 0.0)
                x_after = jnp.where(c < length // n - 1, ref[after, :].astype(F32)[0:1], 0.0)
                prev = jnp.where(row == 0, x_before, pltpu.roll(x, 1, 0))
                nxt = jnp.where(row == n - 1, x_after, pltpu.roll(x, n - 1, 0))
                return sw_ref[0:1, cols] * prev + sw_ref[1:2, cols] * x + sw_ref[2:3, cols] * nxt + sb_ref[:, cols]

            u = short_conv(v_ref, 0) * short_conv(x1_ref, 2)
            x0s_ref[pl.ds(r0, n), :] = short_conv(x0_ref, 1).astype(BF16)
            u_ref[pl.ds(r0, n), :] = u.astype(BF16)
            acc_ref[pl.ds(r0, n), :] = u * skip_ref[...]
            return carry

        lax.fori_loop(0, length // n, chunk, 0)
        nyq_ref[...] = _dot(sg_ref[...], u_ref[...]) * kn_ref[...]

    ub = u_ref[...]
    ur = _dot(cr_ref[...], ub)
    ui = _dot(sr_ref[...], ub)
    kr = kr_ref[...]
    ki = ki_ref[...]
    yr = (ur * kr + ui * ki).astype(BF16)
    yi = (ui * kr - ur * ki).astype(BF16)
    acc_ref[...] += _dot(cc_ref[...], yr) + _dot(sc_ref[...], yi)

    @pl.when(f == pl.num_programs(1) - 1)
    def _():
        sign = (1 - 2 * (row & 1)).astype(F32)
        nyq = sign * nyq_ref[0:1, :]

        def chunk(c, carry):
            rows = pl.ds(pl.multiple_of(c * n, n), n)
            o_ref[rows, :] = (x0s_ref[rows, :].astype(F32) * (acc_ref[rows, :] + nyq)).astype(BF16)
            return carry

        lax.fori_loop(0, length // n, chunk, 0)


def _hyena(p, layer, short_w, short_b, skip, tables, spectrum):
    nb, length, _ = p.shape
    cos_t, sin_t, sign_rows = tables
    kr, ki, kn = spectrum
    tf = min(512, length)
    col = lambda j: pl.BlockSpec((None, length, E_HY), lambda b, f: (b, 0, j))
    return pl.pallas_call(
        functools.partial(_hyena_kernel, length=length),
        out_shape=jax.ShapeDtypeStruct((nb, length, E_HY), BF16),
        grid=(nb, length // tf),
        in_specs=[
            col(COL_HY_V), col(COL_HY_X0), col(COL_HY_X1),
            pl.BlockSpec((None, 3, 3 * E_HY), lambda b, f: (layer, 0, 0)),
            pl.BlockSpec((None, 1, 3 * E_HY), lambda b, f: (layer, 0, 0)),
            pl.BlockSpec((None, 1, E_HY), lambda b, f: (layer, 0, 0)),
            pl.BlockSpec((SUBLANES, length), lambda b, f: (0, 0)),
            pl.BlockSpec((SUBLANES, E_HY), lambda b, f: (0, 0)),
            pl.BlockSpec((tf, length), lambda b, f: (f, 0)),
            pl.BlockSpec((tf, length), lambda b, f: (f, 0)),
            pl.BlockSpec((length, tf), lambda b, f: (0, f)),
            pl.BlockSpec((length, tf), lambda b, f: (0, f)),
            pl.BlockSpec((tf, E_HY), lambda b, f: (f, 0)),
            pl.BlockSpec((tf, E_HY), lambda b, f: (f, 0)),
        ],
        out_specs=pl.BlockSpec((None, length, E_HY), lambda b, f: (b, 0, 0)),
        scratch_shapes=[pltpu.VMEM((length, E_HY), BF16), pltpu.VMEM((length, E_HY), BF16),
                        pltpu.VMEM((length, E_HY), F32), pltpu.VMEM((SUBLANES, E_HY), F32)],
        compiler_params=_cparams("parallel", "arbitrary"),
        name="hyena",
    )(p, p, p, short_w, short_b, skip, sign_rows, kn, cos_t, sin_t, cos_t, sin_t, kr, ki)


LRU_PAD = SUBLANES
LRU_CHUNK = 128


def _lru_kernel(xc_ref, gc_ref, xl_ref, gl_ref, cw_ref, cb_ref, wg_ref, bg_ref, lam_ref, yc_ref, yl_ref,
                xpad, a_cum, b_cum, h_sum, *, tc, tl):
    off_c = LRU_PAD
    off_l = off_c + tc + LRU_PAD
    n = LRU_CHUNK
    n_chunks_c = tc // n
    n_chunks = (tc + tl) // n
    groups_c = tc // SUBLANES
    groups = (tc + tl) // SUBLANES

    xpad[...] = jnp.zeros_like(xpad)
    xpad[off_c:off_c + tc, :] = xc_ref[...].astype(F32)
    xpad[off_l:off_l + tl, :] = xl_ref[...].astype(F32)

    sub = lax.broadcasted_iota(jnp.int32, (n // SUBLANES, SUBLANES, LANES), 1)

    for d in range(2):
        lam = lam_ref[d:d + 1, :]
        softplus_neg = jnp.maximum(-lam, 0.0) + jnp.log1p(jnp.exp(-jnp.abs(lam)))

        def phase_a(k, carry, d=d, softplus_neg=softplus_neg):
            start = pl.multiple_of(off_c + k * n + jnp.where(k >= n_chunks_c, LRU_PAD, 0), SUBLANES)
            for j in range(E_LRU // LANES):
                cols = slice(j * LANES, (j + 1) * LANES)
                win = xpad[pl.ds(start - LRU_PAD, n + 2 * LRU_PAD), cols]
                xc = cb_ref[d:d + 1, cols] + jnp.zeros((n, LANES), F32)
                for tap in range(LRU_CONV):
                    back = LRU_CONV - 1 - tap
                    shift = back if d == 0 else (n + 2 * LRU_PAD - back) % (n + 2 * LRU_PAD)
                    src = win if back == 0 else pltpu.roll(win, shift, 0)
                    xc = xc + cw_ref[d, tap:tap + 1, cols] * src[LRU_PAD:LRU_PAD + n]
                gates = _dot(xc.astype(BF16), wg_ref[d, j]) + bg_ref[d, j:j + 1, :]
                r = jax.nn.sigmoid(gates[:, :LANES])
                i = jax.nn.sigmoid(gates[:, LANES:])
                log_a = (-LRU_C) * r * softplus_neg[:, cols]
                a = jnp.exp(log_a)
                b = jnp.sqrt(-jnp.tanh(log_a) * (a * a + 1.0)) * (i * xc)
                a = a.reshape(n // SUBLANES, SUBLANES, LANES)
                b = b.reshape(n // SUBLANES, SUBLANES, LANES)
                for s in (1, 2, 4):
                    if d == 0:
                        keep = sub >= s
                        a_prev = pltpu.roll(a, s, 1)
                        b_prev = pltpu.roll(b, s, 1)
                    else:
                        keep = sub < SUBLANES - s
                        a_prev = pltpu.roll(a, SUBLANES - s, 1)
                        b_prev = pltpu.roll(b, SUBLANES - s, 1)
                    b = jnp.where(keep, a * b_prev + b, b)
                    a = jnp.where(keep, a * a_prev, a)
                a_cum[pl.ds(start, n), cols] = a.reshape(n, LANES)
                b_cum[pl.ds(start, n), cols] = b.reshape(n, LANES)
            return carry

        lax.fori_loop(0, n_chunks, phase_a, 0)

        def phase_b(g, h_prev, d=d):
            if d == 0:
                row = off_c + g * SUBLANES + jnp.where(g >= groups_c, LRU_PAD, 0)
            else:
                row = jnp.where(g < groups_c, off_c + (groups_c - 1 - g) * SUBLANES,
                                off_l + (groups - 1 - g) * SUBLANES)
            rows = pl.ds(pl.multiple_of(row, SUBLANES), SUBLANES)
            h = a_cum[rows, :] * h_prev + b_cum[rows, :]
            if d == 0:
                h_sum[rows, :] = h
                return jnp.broadcast_to(h[SUBLANES - 1:SUBLANES, :], (SUBLANES, E_LRU))
            h_sum[rows, :] = h_sum[rows, :] + h
            return jnp.broadcast_to(h[0:1, :], (SUBLANES, E_LRU))

        lax.fori_loop(0, groups, phase_b, jnp.zeros((SUBLANES, E_LRU), F32), unroll=4)

    yc_ref[...] = (h_sum[off_c:off_c + tc, :] * _gelu_tanh(gc_ref[...].astype(F32))).astype(BF16)

    def phase_c(k, carry):
        r0 = pl.multiple_of(k * 256, 256)
        yl_ref[pl.ds(r0, 256), :] = (h_sum[pl.ds(off_l + r0, 256), :]
                                     * _gelu_tanh(gl_ref[pl.ds(r0, 256), :].astype(F32))).astype(BF16)
        return carry

    lax.fori_loop(0, tl // 256, phase_c, 0)


def _lru_gate_weights(wr, wi, br, bi):
    per = LANES // LRU_BLOCK_DIM
    nj = E_LRU // LANES

    def lane_blocks(w):
        w = w.reshape(DEPTH, 2, nj, per, LRU_BLOCK_DIM, LRU_BLOCK_DIM)
        eye = jnp.eye(per, dtype=w.dtype)
        full = w[:, :, :, :, :, None, :] * eye[None, None, None, :, None, :, None]
        return full.reshape(DEPTH, 2, nj, LANES, LANES)

    wg = jnp.concatenate([lane_blocks(wr), lane_blocks(wi)], axis=-1).astype(BF16)
    bg = jnp.concatenate([br.reshape(DEPTH, 2, nj, LANES), bi.reshape(DEPTH, 2, nj, LANES)], axis=-1)
    return wg, bg


def _lru(p_ctx, p_lat, layer, conv_w, conv_b, wg, bg, lam):
    nb, tc, _ = p_ctx.shape
    tl = p_lat.shape[1]
    rows = tc + tl + 3 * LRU_PAD
    col = lambda t, j: pl.BlockSpec((None, t, E_LRU), lambda b: (b, 0, j))
    par = lambda shape: pl.BlockSpec((None,) + shape, lambda b: (layer,) + (0,) * len(shape))
    return pl.pallas_call(
        functools.partial(_lru_kernel, tc=tc, tl=tl),
        out_shape=[jax.ShapeDtypeStruct((nb, tc, E_LRU), BF16), jax.ShapeDtypeStruct((nb, tl, E_LRU), BF16)],
        grid=(nb,),
        in_specs=[col(tc, COL_LRU_X), col(tc, COL_LRU_G), col(tl, COL_LRU_X), col(tl, COL_LRU_G),
                  par((2, LRU_CONV, E_LRU)), par((2, E_LRU)), par((2, E_LRU // LANES, LANES, 2 * LANES)),
                  par((2, E_LRU // LANES, 2 * LANES)), par((2, E_LRU))],
        out_specs=[pl.BlockSpec((None, tc, E_LRU), lambda b: (b, 0, 0)),
                   pl.BlockSpec((None, tl, E_LRU), lambda b: (b, 0, 0))],
        scratch_shapes=[pltpu.VMEM((rows, E_LRU), F32)] * 4,
        compiler_params=_cparams("parallel"),
        name="lru",
    )(p_ctx, p_ctx, p_lat, p_lat, conv_w, conv_b, wg, bg, lam)


def _cumsum_chunk(x, sub, reverse):
    groups = x.shape[0] // SUBLANES
    x3 = x.reshape(groups, SUBLANES, x.shape[1])
    for s in (1, 2, 4):
        if reverse:
            x3 = x3 + jnp.where(sub < SUBLANES - s, pltpu.roll(x3, SUBLANES - s, 1), 0.0)
        else:
            x3 = x3 + jnp.where(sub >= s, pltpu.roll(x3, s, 1), 0.0)
    edge = 0 if reverse else SUBLANES - 1
    blocks = [None] * groups
    total = None
    for g in (range(groups - 1, -1, -1) if reverse else range(groups)):
        blk = x3[g] if total is None else x3[g] + total
        blocks[g] = blk
        total = blk[edge:edge + 1, :]
    return blocks


def _hgrn_kernel(qc_ref, ffc_ref, fbc_ref, ic_ref, ogc_ref, ql_ref, ffl_ref, fbl_ref, il_ref, ogl_ref,
                 lb_ref, ng_ref, yc_ref, yl_ref, o_acc, o_f, o_b, qg_s, kd_s, v_s, dec_s, *, layer, tc, tl):
    ch = HG_CHUNK
    raw = [lb_ref[i:i + 1, :] for i in range(DEPTH)]
    top = functools.reduce(jnp.maximum, raw)
    ex = [jnp.exp(r - top) for r in raw]
    tot = functools.reduce(lambda a, b: a + b, ex)
    lb = jnp.zeros((1, E_HG), F32)
    for i in range(1, layer + 1):
        lb = lb + ex[i] / tot

    rt = lax.broadcasted_iota(jnp.int32, (ch, ch), 0)
    ct = lax.broadcasted_iota(jnp.int32, (ch, ch), 1)
    tri = [(rt >= ct), (rt <= ct)]
    sub = lax.broadcasted_iota(jnp.int32, (ch // SUBLANES, SUBLANES, HG_DK), 1)
    n_c, n_l = tc // ch, tl // ch
    n_all = n_c + n_l
    mid_g = (ch // 2) // SUBLANES
    last = SUBLANES - 1

    def intra(q_ref, ff_ref, fb_ref, i_ref, n_chunks, chunk0):
        def body(c, carry):
            rows = pl.ds(pl.multiple_of(c * ch, ch), ch)
            srows = pl.ds(pl.multiple_of((chunk0 + c) * ch, ch), ch)
            drows = pl.ds(pl.multiple_of((chunk0 + c) * SUBLANES, SUBLANES), SUBLANES)
            for hd in range(HG_HEADS):
                cols = slice(hd * HG_DK, (hd + 1) * HG_DK)
                lbh = lb[:, cols]
                q = _silu(q_ref[rows, cols].astype(F32))
                v = i_ref[rows, cols]
                v_s[srows, cols] = v
                o = None
                for d, f_ref in enumerate((ff_ref, fb_ref)):
                    f = lbh + (1.0 - lbh) * jax.nn.sigmoid(f_ref[rows, cols].astype(F32))
                    k = 1.0 - f
                    blocks = _cumsum_chunk(jnp.log(f), sub, reverse=d == 1)
                    b = jnp.concatenate(blocks, axis=0)
                    if d == 0:
                        b_last, b_mid = blocks[-1][last:last + 1], blocks[mid_g][0:1]
                    else:
                        b_last, b_mid = blocks[0][0:1], blocks[mid_g - 1][last:last + 1]
                    e_mid = jnp.exp(b - b_mid)
                    qs = (q * e_mid).astype(BF16)
                    ks = (k * jnp.exp(b_mid - b)).astype(BF16)
                    qg_s[d, srows, cols] = (q * e_mid * jnp.exp(b_mid)).astype(BF16)
                    kd_s[d, srows, cols] = (k * jnp.exp(b_last - b)).astype(BF16)
                    dec_s[d, drows, cols] = jnp.broadcast_to(jnp.exp(b_last), (SUBLANES, HG_DK))
                    scores = jnp.where(tri[d], _dot_nt(qs, ks), 0.0)
                    od = _dot(scores.astype(BF16), v)
                    o = od if o is None else o + od
                o_acc[srows, cols] = o
            return carry

        lax.fori_loop(0, n_chunks, body, 0, unroll=2)

    intra(qc_ref, ffc_ref, fbc_ref, ic_ref, n_c, 0)
    intra(ql_ref, ffl_ref, fbl_ref, il_ref, n_l, n_c)

    def inter(it, states):
        new = []
        for d in range(2):
            g = it if d == 0 else jnp.where(it < n_c, n_c - 1 - it, n_all + n_c - 1 - it)
            rows = pl.ds(pl.multiple_of(g * ch, ch), ch)
            drows = pl.ds(pl.multiple_of(g * SUBLANES, SUBLANES), SUBLANES)
            o_d = o_f if d == 0 else o_b
            for hd in range(HG_HEADS):
                cols = slice(hd * HG_DK, (hd + 1) * HG_DK)
                s_t = states[d * HG_HEADS + hd]
                o_d[rows, cols] = _dot_nt(qg_s[d, rows, cols], s_t.astype(BF16))
                dec = dec_s[d, drows, cols][0:1]
                new.append(s_t * dec + _dot_tn(v_s[rows, cols], kd_s[d, rows, cols]))
        return tuple(new)

    lax.fori_loop(0, n_all, inter, tuple(jnp.zeros((HG_DV, HG_DK), F32) for _ in range(2 * HG_HEADS)), unroll=2)

    def finish(og_ref, y_ref, base, t):
        blk = min(t, 256)

        def body(k, carry):
            r0 = pl.multiple_of(k * blk, blk)
            for hd in range(HG_HEADS):
                cols = slice(hd * HG_DV, (hd + 1) * HG_DV)
                orow = pl.ds(base + r0, blk)
                o = o_acc[orow, cols] + o_f[orow, cols] + o_b[orow, cols]
                y = _rms(o) * ng_ref[...] * _silu(og_ref[pl.ds(r0, blk), cols].astype(F32))
                y_ref[pl.ds(r0, blk), cols] = y.astype(BF16)
            return carry

        lax.fori_loop(0, t // blk, body, 0)

    finish(ogc_ref, yc_ref, 0, tc)
    finish(ogl_ref, yl_ref, tc, tl)


def _hgrn(p_ctx, p_lat, layer, lower_bounds, norm_g):
    nb, tc, _ = p_ctx.shape
    tl = p_lat.shape[1]
    t = tc + tl
    col = lambda n, j: pl.BlockSpec((None, n, E_HG), lambda b: (b, 0, j), pipeline_mode=pl.Buffered(1))
    names = (COL_HG_Q, COL_HG_FF, COL_HG_FB, COL_HG_I, COL_HG_OG)
    return pl.pallas_call(
        functools.partial(_hgrn_kernel, layer=layer, tc=tc, tl=tl),
        out_shape=[jax.ShapeDtypeStruct((nb, tc, E_HG), BF16), jax.ShapeDtypeStruct((nb, tl, E_HG), BF16)],
        grid=(nb,),
        in_specs=[col(tc, j) for j in names] + [col(tl, j) for j in names] + [
            pl.BlockSpec((DEPTH, E_HG), lambda b: (0, 0)),
            pl.BlockSpec((None, 1, HG_DV), lambda b: (layer, 0, 0)),
        ],
        out_specs=[pl.BlockSpec((None, tc, E_HG), lambda b: (b, 0, 0)),
                   pl.BlockSpec((None, tl, E_HG), lambda b: (b, 0, 0))],
        scratch_shapes=[pltpu.VMEM((t, E_HG), F32), pltpu.VMEM((t, E_HG), F32), pltpu.VMEM((t, E_HG), F32),
                        pltpu.VMEM((2, t, E_HG), BF16), pltpu.VMEM((2, t, E_HG), BF16), pltpu.VMEM((t, E_HG), BF16),
                        pltpu.VMEM((2, t // HG_CHUNK * SUBLANES, E_HG), F32)],
        compiler_params=_cparams("parallel"),
        name="hgrn",
    )(*([p_ctx] * 5 + [p_lat] * 5 + [lower_bounds, norm_g]))


def _merge_kernel(gh_ref, gl_ref, gg_ref, yh_ref, yl_ref, yg_ref, wh_ref, wl_ref, wg_ref, wo_ref,
                  h_ref, g1_ref, gain_ref, o_ref, *t_ref, n_cols, n_rows):
    def branch(g_ref, y_ref, w_ref):
        return jax.nn.sigmoid(g_ref[...].astype(F32)) * _dot(y_ref[...], w_ref[...])

    m = branch(gh_ref, yh_ref, wh_ref) + branch(gl_ref, yl_ref, wl_ref) + branch(gg_ref, yg_ref, wg_ref)
    r = _rms(_dot(m.astype(BF16), wo_ref[...])) * (gain_ref[...] * g1_ref[...])
    if n_cols:
        for j in range(D_MODEL // LANES):
            lanes = slice(j * LANES, (j + 1) * LANES)
            for c in range(n_cols):
                t_ref[0][j, pl.ds(c, n_rows, stride=n_cols), :] = r[c * n_rows:(c + 1) * n_rows, lanes]
            o_ref[:, :, lanes] = h_ref[:, :, lanes] + t_ref[0][j].reshape(n_rows, n_cols, LANES)
    else:
        o_ref[...] = h_ref[...] + r


def _merge(p, y_hy, y_lru, y_hg, w_hy, w_lru, w_hg, w_out, h, ada4, gains4, layer, *, ctx_row, col_major):
    nb, t, _ = h.shape
    tm = 512
    row_fn = (lambda b: b) if ctx_row is None else (lambda b: ctx_row)
    if col_major:
        n_rows = t // GRID_W
        n_cols = tm // n_rows
        h = h.reshape(nb, n_rows, GRID_W, D_MODEL)
        h_spec = pl.BlockSpec((None, n_rows, n_cols, D_MODEL), lambda b, i: (b, 0, i, 0))
        scratch = [pltpu.VMEM((D_MODEL // LANES, tm, LANES), F32)]
    else:
        n_rows = n_cols = 0
        h_spec = pl.BlockSpec((None, tm, D_MODEL), lambda b, i: (b, i, 0))
        scratch = []
    gate = lambda k: pl.BlockSpec((None, tm, D_MODEL), lambda b, i: (b, i, COL_GATES + k))
    y_spec = pl.BlockSpec((None, tm, E_HY), lambda b, i: (b, i, 0))
    w_spec = pl.BlockSpec((None, E_HY, D_MODEL), lambda b, i: (layer, 0, 0))
    out = pl.pallas_call(
        functools.partial(_merge_kernel, n_cols=n_cols, n_rows=n_rows),
        out_shape=jax.ShapeDtypeStruct(h.shape, F32),
        grid=(nb, t // tm),
        in_specs=[gate(0), gate(1), gate(2), y_spec, y_spec, y_spec, w_spec, w_spec, w_spec,
                  pl.BlockSpec((None, D_MODEL, D_MODEL), lambda b, i: (layer, 0, 0)),
                  h_spec, _ada_spec(layer, row_fn, 2, 2), _gain_spec(layer, 1, 2)],
        out_specs=h_spec,
        scratch_shapes=scratch,
        compiler_params=_cparams("parallel", "parallel"),
        name="merge",
    )(p, p, p, y_hy, y_lru, y_hg, w_hy, w_lru, w_hg, w_out, h, ada4, gains4)
    return out.reshape(nb, t, D_MODEL)


def _mlp_kernel(h_ref, sh_ref, sc_ref, g2_ref, gain2_ref, gain3_ref, w1_ref, w2_ref, o_ref, u_ref, acc_ref):
    k = pl.program_id(2)

    n_trips = u_ref.shape[0] // NORM_ROWS

    @pl.when(k == 0)
    def _():
        mul = gain2_ref[...] * (1.0 + sc_ref[...])
        sh = sh_ref[...]

        def row_chunk(i, carry):
            rows = pl.ds(pl.multiple_of(i * NORM_ROWS, NORM_ROWS), NORM_ROWS)
            u_ref[rows, :] = (_rms(h_ref[rows, :]) * mul + sh).astype(BF16)
            return carry

        lax.fori_loop(0, n_trips, row_chunk, 0)
        acc_ref[...] = jnp.zeros_like(acc_ref)

    a = jnp.maximum(_dot(u_ref[...], w1_ref[...]), 0.0)
    acc_ref[...] += _dot((a * a).astype(BF16), w2_ref[...])

    @pl.when(k == pl.num_programs(2) - 1)
    def _():
        mul = g2_ref[...] * gain3_ref[...]

        def row_chunk(i, carry):
            rows = pl.ds(pl.multiple_of(i * NORM_ROWS, NORM_ROWS), NORM_ROWS)
            o_ref[rows, :] = h_ref[rows, :] + mul * _rms(acc_ref[rows, :])
            return carry

        lax.fori_loop(0, n_trips, row_chunk, 0)


def _mlp(h, ada4, gains4, w1_b, w2_b, layer, *, ctx_row):
    nb, t, _ = h.shape
    tm, tf = min(1024, t), 1024
    row_fn = (lambda b: b) if ctx_row is None else (lambda b: ctx_row)
    h_spec = pl.BlockSpec((None, tm, D_MODEL), lambda b, i, k: (b, i, 0))
    return pl.pallas_call(
        _mlp_kernel,
        out_shape=jax.ShapeDtypeStruct(h.shape, F32),
        grid=(nb, t // tm, D_FF // tf),
        in_specs=[h_spec, _ada_spec(layer, row_fn, 3, 3), _ada_spec(layer, row_fn, 4, 3),
                  _ada_spec(layer, row_fn, 5, 3), _gain_spec(layer, 2, 3), _gain_spec(layer, 3, 3),
                  pl.BlockSpec((None, D_MODEL, tf), lambda b, i, k: (layer, 0, k)),
                  pl.BlockSpec((None, tf, D_MODEL), lambda b, i, k: (layer, k, 0))],
        out_specs=h_spec,
        scratch_shapes=[pltpu.VMEM((tm, D_MODEL), BF16), pltpu.VMEM((tm, D_MODEL), F32)],
        compiler_params=_cparams("parallel", "parallel", "arbitrary"),
        name="mlp",
    )(h, ada4, ada4, ada4, gains4, gains4, w1_b, w2_b)


def kernel(x, c, ctx, c_ctx, w_ada, b_ada, norm_gains, w_in, hy_short_w, hy_short_b, hy_ff_w1, hy_ff_b1,
           hy_ff_w2, hy_ff_b2, hy_ff_w3, hy_freq, hy_skip, lru_conv_w, lru_conv_b, lru_wr, lru_br, lru_wi,
           lru_bi, lru_lambda, hg_lower_bounds, hg_norm_g, w_proj_hy, w_proj_lru, w_proj_hg, w_out,
           w_mlp1, w_mlp2):
    nb, seq, _ = x.shape
    ctx_len = ctx.shape[1]
    assert seq % GRID_W == 0

    w_in_b, w_out_b = w_in.astype(BF16), w_out.astype(BF16)
    w_hy_b, w_lru_b, w_hg_b = w_proj_hy.astype(BF16), w_proj_lru.astype(BF16), w_proj_hg.astype(BF16)
    w1_b, w2_b = w_mlp1.astype(BF16), w_mlp2.astype(BF16)

    ctx_row = nb
    ada_rows = -(-(nb + 1) // SUBLANES) * SUBLANES
    cin = jnp.concatenate([c, c_ctx[None], jnp.zeros((ada_rows - nb - 1, D_MODEL), F32)], axis=0)
    ada4 = _ada_call(cin, w_ada.astype(BF16), b_ada).reshape(DEPTH, ada_rows, 1, N_ADA * D_MODEL)
    gains4 = norm_gains.reshape(DEPTH, 4, 1, D_MODEL)

    hy_w1p = jnp.pad(hy_ff_w1, ((0, 0), (0, LANES - hy_ff_w1.shape[1]), (0, 0)))
    hy_b1 = hy_ff_b1.reshape(DEPTH, 1, HY_ORDER)
    hy_b2 = hy_ff_b2.reshape(DEPTH, 1, HY_ORDER)
    hy_sb = hy_short_b.reshape(DEPTH, 1, 3 * E_HY)
    hy_sk = hy_skip.reshape(DEPTH, 1, E_HY)
    lru_wg, lru_bg = _lru_gate_weights(lru_wr, lru_wi, lru_br, lru_bi)
    hg_ng = hg_norm_g.reshape(DEPTH, 1, HG_DV)

    tables = {seq: _dft_tables(seq), ctx_len: _dft_tables(ctx_len)}

    def hyena(p, layer):
        length = p.shape[1]
        hs, hd = _hy_taps(length, layer, hy_w1p, hy_b1, hy_ff_w2, hy_b2, hy_ff_w3, hy_freq)
        spectrum = _hy_spectrum(length, *tables[length], hs, hd)
        return _hyena(p, layer, hy_short_w, hy_sb, hy_sk, tables[length], spectrum)

    h_lat = x
    h_ctx = ctx.reshape(1, nb * ctx_len, D_MODEL)
    for layer in range(DEPTH):
        need_ctx = layer < DEPTH - 1
        col_major = layer % 2 == 1
        p_lat = _in_proj(h_lat, ada4, gains4, w_in_b, layer, ctx_row=None, col_major=col_major)
        p_ctx_flat = _in_proj(h_ctx, ada4, gains4, w_in_b, layer, ctx_row=ctx_row, col_major=False)
        p_ctx = p_ctx_flat.reshape(nb, ctx_len, IN_WIDTH)

        y_hy_lat = hyena(p_lat, layer)
        y_lru_ctx, y_lru_lat = _lru(p_ctx, p_lat, layer, lru_conv_w, lru_conv_b, lru_wg, lru_bg, lru_lambda)
        y_hg_ctx, y_hg_lat = _hgrn(p_ctx, p_lat, layer, hg_lower_bounds, hg_ng)
        proj = (w_hy_b, w_lru_b, w_hg_b, w_out_b)
        h_lat = _merge(p_lat, y_hy_lat, y_lru_lat, y_hg_lat, *proj, h_lat, ada4, gains4, layer,
                       ctx_row=None, col_major=col_major)
        if need_ctx:
            flat = lambda y: y.reshape(1, nb * ctx_len, y.shape[-1])
            y_hy_ctx = hyena(p_ctx, layer)
            h_ctx = _merge(p_ctx_flat, flat(y_hy_ctx), flat(y_lru_ctx), flat(y_hg_ctx), *proj, h_ctx, ada4,
                           gains4, layer, ctx_row=ctx_row, col_major=False)

        h_lat = _mlp(h_lat, ada4, gains4, w1_b, w2_b, layer, ctx_row=None)
        if need_ctx:
            h_ctx = _mlp(h_ctx, ada4, gains4, w1_b, w2_b, layer, ctx_row=ctx_row)
    return h_lat
```

```python
import functools
import math

import jax
import jax.numpy as jnp
from jax import lax
from jax.experimental import pallas as pl
from jax.experimental.pallas import tpu as pltpu

F32 = jnp.float32
BF16 = jnp.bfloat16
HIGHEST = lax.Precision.HIGHEST

D_MODEL = 1024
DEPTH = 2
GRID_W = 64
EPS = 1e-6

E_HY = 512
HY_EMB_BANDS = 8
HY_ORDER = 64
HY_DECAY_TARGET = 1e-2
HY_FAST_DECAY = 0.3
HY_SLOW_DECAY = 1.5
HY_ROWS = 64
DFT_COARSE = 64

E_LRU = 512
LRU_BLOCKS = 8
LRU_BLOCK_DIM = E_LRU // LRU_BLOCKS
LRU_CONV = 4
LRU_C = 8.0

HG_HEADS = 4
HG_DK = 128
HG_DV = 128
E_HG = HG_HEADS * HG_DV
HG_CHUNK = 64

D_FF = 4 * D_MODEL
IN_WIDTH = 8 * D_MODEL
N_ADA = 6

COL_HY_V, COL_HY_X0, COL_HY_X1 = 0, 1, 2
COL_LRU_X, COL_LRU_G = 3, 4
COL_HG_Q, COL_HG_FF, COL_HG_FB, COL_HG_I, COL_HG_OG = 5, 6, 7, 8, 9
COL_GATES = 5

LANES = 128
SUBLANES = 8
VMEM_LIMIT = 56 * 1024 * 1024
NORM_ROWS = 512
MLP_FF_BLOCK = 1024


def _cparams(*sem):
    return pltpu.CompilerParams(dimension_semantics=sem, vmem_limit_bytes=VMEM_LIMIT)


def _rms(x):
    return x * lax.rsqrt(jnp.mean(x * x, axis=-1, keepdims=True) + EPS)


def _silu(x):
    return x * jax.nn.sigmoid(x)


def _gelu_tanh(x):
    return x * (0.5 * (1.0 + jnp.tanh(math.sqrt(2.0 / math.pi) * (x + 0.044715 * (x * x * x)))))


def _dot(a, b):
    return jnp.dot(a, b, preferred_element_type=F32)


def _dot_nt(a, b):
    return lax.dot_general(a, b, (((1,), (1,)), ((), ())), preferred_element_type=F32)


def _dot_tn(a, b):
    return lax.dot_general(a, b, (((0,), (0,)), ((), ())), preferred_element_type=F32)


def _ada_spec(layer, row_fn, chunk, nargs):
    if nargs == 2:
        return pl.BlockSpec((None, None, 1, D_MODEL), lambda b, i: (layer, row_fn(b), 0, chunk))
    return pl.BlockSpec((None, None, 1, D_MODEL), lambda b, i, j: (layer, row_fn(b), 0, chunk))


def _gain_spec(layer, idx, nargs):
    if nargs == 2:
        return pl.BlockSpec((None, None, 1, D_MODEL), lambda b, i: (layer, idx, 0, 0))
    return pl.BlockSpec((None, None, 1, D_MODEL), lambda b, i, j: (layer, idx, 0, 0))


def _ada_kernel(c_ref, w_ref, b_ref, o_ref):
    s = _silu(c_ref[...])
    o_ref[...] = _dot(s.astype(BF16), w_ref[...]) + b_ref[...]


def _ada_call(cin, w_ada_b, b_ada):
    tn = D_MODEL
    rows = cin.shape[0]
    return pl.pallas_call(
        _ada_kernel,
        out_shape=jax.ShapeDtypeStruct((DEPTH, rows, N_ADA * D_MODEL), F32),
        grid=(DEPTH, N_ADA * D_MODEL // tn),
        in_specs=[
            pl.BlockSpec((rows, D_MODEL), lambda l, j: (0, 0)),
            pl.BlockSpec((None, D_MODEL, tn), lambda l, j: (l, 0, j)),
            pl.BlockSpec((None, 1, tn), lambda l, j: (l, 0, j)),
        ],
        out_specs=pl.BlockSpec((None, rows, tn), lambda l, j: (l, 0, j)),
        compiler_params=_cparams("parallel", "parallel"),
        name="ada",
    )(cin, w_ada_b, b_ada.reshape(DEPTH, 1, N_ADA * D_MODEL))


def _inproj_kernel(h_ref, sh_ref, sc_ref, g_ref, w_ref, o_ref, u_ref, *t_ref, n_cols, n_rows):
    @pl.when(pl.program_id(2) == 0)
    def _():
        mul = g_ref[...] * (1.0 + sc_ref[...])
        sh = sh_ref[...]
        if n_cols:
            def grid_row(r, carry):
                y = _rms(h_ref[r]) * mul + sh
                rows = pl.ds(pl.multiple_of(r * n_cols, n_cols), n_cols)
                for j in range(D_MODEL // LANES):
                    t_ref[0][j, rows, :] = y[:, j * LANES:(j + 1) * LANES]
                return carry

            lax.fori_loop(0, n_rows, grid_row, 0, unroll=4)
            for j in range(D_MODEL // LANES):
                lanes = slice(j * LANES, (j + 1) * LANES)
                for c in range(n_cols):
                    u_ref[c * n_rows:(c + 1) * n_rows, lanes] = (
                        t_ref[0][j, pl.ds(c, n_rows, stride=n_cols), :].astype(BF16))
        else:
            for k in range(0, u_ref.shape[0], NORM_ROWS):
                u_ref[k:k + NORM_ROWS, :] = (_rms(h_ref[k:k + NORM_ROWS, :]) * mul + sh).astype(BF16)

    o_ref[...] = _dot(u_ref[...], w_ref[...]).astype(BF16)


def _in_proj(h, ada4, gains4, w_in_b, layer, *, ctx_row, col_major):
    nb, t, _ = h.shape
    tm, tn = min(2048, t), 1024
    row_fn = (lambda b: b) if ctx_row is None else (lambda b: ctx_row)
    if col_major:
        n_rows = t // GRID_W
        n_cols = tm // n_rows
        h = h.reshape(nb, n_rows, GRID_W, D_MODEL)
        h_spec = pl.BlockSpec((None, n_rows, n_cols, D_MODEL), lambda b, i, j: (b, 0, i, 0))
        scratch = [pltpu.VMEM((tm, D_MODEL), BF16), pltpu.VMEM((D_MODEL // LANES, tm, LANES), F32)]
    else:
        n_rows = n_cols = 0
        h_spec = pl.BlockSpec((None, tm, D_MODEL), lambda b, i, j: (b, i, 0))
        scratch = [pltpu.VMEM((tm, D_MODEL), BF16)]
    return pl.pallas_call(
        functools.partial(_inproj_kernel, n_cols=n_cols, n_rows=n_rows),
        out_shape=jax.ShapeDtypeStruct((nb, t, IN_WIDTH), BF16),
        grid=(nb, t // tm, IN_WIDTH // tn),
        in_specs=[
            h_spec,
            _ada_spec(layer, row_fn, 0, 3),
            _ada_spec(layer, row_fn, 1, 3),
            _gain_spec(layer, 0, 3),
            pl.BlockSpec((None, D_MODEL, tn), lambda b, i, j: (layer, 0, j)),
        ],
        out_specs=pl.BlockSpec((None, tm, tn), lambda b, i, j: (b, i, j)),
        scratch_shapes=scratch,
        compiler_params=_cparams("parallel", "parallel", "arbitrary"),
        name="in_proj",
    )(h, ada4, ada4, gains4, w_in_b)


def _dft_tables(length):
    n = 2 * length
    idx = jnp.arange(length, dtype=jnp.int32)

    def trig(freqs):
        ang = ((freqs[:, None] * idx[None, :]) % n).astype(F32) * (2.0 * math.pi / n)
        return jnp.cos(ang), jnp.sin(ang)

    c_hi, s_hi = trig(jnp.arange(length // DFT_COARSE, dtype=jnp.int32) * DFT_COARSE)
    c_lo, s_lo = trig(jnp.arange(DFT_COARSE, dtype=jnp.int32))
    cos_t = (c_hi[:, None] * c_lo[None] - s_hi[:, None] * s_lo[None]).reshape(length, length)
    sin_t = (s_hi[:, None] * c_lo[None] + c_hi[:, None] * s_lo[None]).reshape(length, length)
    sign = (1 - 2 * (idx % 2)).astype(F32)
    sign_rows = jnp.zeros((SUBLANES, length), F32).at[0].set(sign)
    return cos_t.astype(BF16), sin_t.astype(BF16), sign_rows.astype(BF16)


def _hy_taps_kernel(w1_ref, b1_ref, w2_ref, b2_ref, w3_ref, fr_ref, hs_ref, hd_ref, *, length, tl):
    row0 = pl.program_id(0) * tl
    pos = (lax.broadcasted_iota(jnp.int32, (tl, LANES), 0) + row0).astype(F32)
    lane = lax.broadcasted_iota(jnp.int32, (tl, LANES), 1)
    t = pos / max(length - 1, 1)
    band_step = (HY_EMB_BANDS - 1 - 1e-4) / (HY_EMB_BANDS - 1)
    band = 1e-4 + jnp.where(lane <= HY_EMB_BANDS, lane - 1, lane - 1 - HY_EMB_BANDS).astype(F32) * band_step
    ang = band * (2.0 * math.pi * pos / length)
    z = jnp.where(lane == 0, t,
                  jnp.where(lane <= HY_EMB_BANDS, jnp.cos(ang),
                            jnp.where(lane <= 2 * HY_EMB_BANDS, -jnp.sin(ang), 0.0)))
    h = jnp.sin(fr_ref[0:1, :] * (jnp.dot(z, w1_ref[...], precision=HIGHEST, preferred_element_type=F32) + b1_ref[...]))
    h = jnp.sin(fr_ref[1:2, :] * (jnp.dot(h, w2_ref[...], precision=HIGHEST, preferred_element_type=F32) + b2_ref[...]))
    h = jnp.dot(h, w3_ref[...], precision=HIGHEST, preferred_element_type=F32)
    ch = lax.broadcasted_iota(jnp.int32, (1, E_HY), 1).astype(F32)
    lo = math.log(HY_DECAY_TARGET) / HY_SLOW_DECAY
    hi = math.log(HY_DECAY_TARGET) / HY_FAST_DECAY
    delta = jnp.abs(lo + ch * ((hi - lo) / (E_HY - 1)))
    rows = lax.broadcasted_iota(jnp.int32, (tl, E_HY), 0) + row0
    decay = jnp.exp(-(rows.astype(F32) / max(length - 1, 1)) * delta)
    h_fwd = h[:, :E_HY] * decay
    h_bwd = jnp.where(rows == 0, 0.0, h[:, E_HY:] * decay)
    hs_ref[...] = (h_fwd + h_bwd).astype(BF16)
    hd_ref[...] = (h_bwd - h_fwd).astype(BF16)


def _hy_taps(length, layer, w1p, b1, w2, b2, w3, freq):
    tl = 256
    full = lambda shape: pl.BlockSpec((None,) + shape, lambda i: (layer,) + (0,) * len(shape))
    return pl.pallas_call(
        functools.partial(_hy_taps_kernel, length=length, tl=tl),
        out_shape=[jax.ShapeDtypeStruct((length, E_HY), BF16)] * 2,
        grid=(length // tl,),
        in_specs=[full((LANES, HY_ORDER)), full((1, HY_ORDER)), full((HY_ORDER, HY_ORDER)), full((1, HY_ORDER)),
                  full((HY_ORDER, 2 * E_HY)), full((2, HY_ORDER))],
        out_specs=[pl.BlockSpec((tl, E_HY), lambda i: (i, 0))] * 2,
        compiler_params=_cparams("parallel"),
        name="hy_taps",
    )(w1p, b1, w2, b2, w3, freq)


def _hy_spec_kernel(c_ref, s_ref, sg_ref, hs_ref, hd_ref, kr_ref, ki_ref, kn_ref, *, length, tf):
    n = 2 * length
    row = lax.broadcasted_iota(jnp.int32, (tf, E_HY), 0) + pl.program_id(0) * tf
    scale = jnp.where(row == 0, 1.0 / n, 2.0 / n)
    kr_ref[...] = _dot(c_ref[...], hs_ref[...]) * scale
    ki_ref[...] = _dot(s_ref[...], hd_ref[...]) * scale
    kn_ref[...] = _dot(sg_ref[...], hs_ref[...]) * (1.0 / n)


def _hy_spectrum(length, cos_t, sin_t, sign_rows, hs, hd):
    tf = 256
    return pl.pallas_call(
        functools.partial(_hy_spec_kernel, length=length, tf=tf),
        out_shape=[jax.ShapeDtypeStruct((length, E_HY), F32), jax.ShapeDtypeStruct((length, E_HY), F32),
                   jax.ShapeDtypeStruct((SUBLANES, E_HY), F32)],
        grid=(length // tf,),
        in_specs=[
            pl.BlockSpec((tf, length), lambda i: (i, 0)),
            pl.BlockSpec((tf, length), lambda i: (i, 0)),
            pl.BlockSpec((SUBLANES, length), lambda i: (0, 0)),
            pl.BlockSpec((length, E_HY), lambda i: (0, 0)),
            pl.BlockSpec((length, E_HY), lambda i: (0, 0)),
        ],
        out_specs=[pl.BlockSpec((tf, E_HY), lambda i: (i, 0)), pl.BlockSpec((tf, E_HY), lambda i: (i, 0)),
                   pl.BlockSpec((SUBLANES, E_HY), lambda i: (0, 0))],
        compiler_params=_cparams("arbitrary"),
        name="hy_spectrum",
    )(cos_t, sin_t, sign_rows, hs, hd)


def _hyena_kernel(v_ref, x0_ref, x1_ref, sw_ref, sb_ref, skip_ref, sg_ref, kn_ref,
                  cr_ref, sr_ref, cc_ref, sc_ref, kr_ref, ki_ref, o_ref, u_ref, x0s_ref, acc_ref, nyq_ref,
                  *, length):
    f = pl.program_id(1)

    n = HY_ROWS
    halo = 2 * SUBLANES
    row = lax.broadcasted_iota(jnp.int32, (n, E_HY), 0)

    @pl.when(f == 0)
    def _():
        def chunk(c, carry):
            r0 = pl.multiple_of(c * n, n)
            before = pl.ds(pl.multiple_of(jnp.maximum(r0 - halo, 0), halo), halo)
            after = pl.ds(pl.multiple_of(jnp.minimum(r0 + n, length - halo), halo), halo)

            def short_conv(ref, j):
                cols = slice(j * E_HY, (j + 1) * E_HY)
                x = ref[pl.ds(r0, n), :].astype(F32)
                x_before = jnp.where(c > 0, ref[before, :].astype(F32)[halo - 1:halo], 0.0)
                x_after = jnp.where(c < length // n - 1, ref[after, :].astype(F32)[0:1], 0.0)
                prev = jnp.where(row == 0, x_before, pltpu.roll(x, 1, 0))
                nxt = jnp.where(row == n - 1, x_after, pltpu.roll(x, n - 1, 0))
                return sw_ref[0:1, cols] * prev + sw_ref[1:2, cols] * x + sw_ref[2:3, cols] * nxt + sb_ref[:, cols]

            u = short_conv(v_ref, 0) * short_conv(x1_ref, 2)
            x0s_ref[pl.ds(r0, n), :] = short_conv(x0_ref, 1).astype(BF16)
            u_ref[pl.ds(r0, n), :] = u.astype(BF16)
            acc_ref[pl.ds(r0, n), :] = u * skip_ref[...]
            return carry

        lax.fori_loop(0, length // n, chunk, 0)
        nyq_ref[...] = _dot(sg_ref[...], u_ref[...]) * kn_ref[...]

    ub = u_ref[...]
    ur = _dot(cr_ref[...], ub)
    ui = _dot(sr_ref[...], ub)
    kr = kr_ref[...]
    ki = ki_ref[...]
    yr = (ur * kr + ui * ki).astype(BF16)
    yi = (ui * kr - ur * ki).astype(BF16)
    acc_ref[...] += _dot(cc_ref[...], yr) + _dot(sc_ref[...], yi)

    @pl.when(f == pl.num_programs(1) - 1)
    def _():
        sign = (1 - 2 * (row & 1)).astype(F32)
        nyq = sign * nyq_ref[0:1, :]

        def chunk(c, carry):
            rows = pl.ds(pl.multiple_of(c * n, n), n)
            o_ref[rows, :] = (x0s_ref[rows, :].astype(F32) * (acc_ref[rows, :] + nyq)).astype(BF16)
            return carry

        lax.fori_loop(0, length // n, chunk, 0)


def _hyena(p, layer, short_w, short_b, skip, tables, spectrum):
    nb, length, _ = p.shape
    cos_t, sin_t, sign_rows = tables
    kr, ki, kn = spectrum
    tf = min(512, length)
    col = lambda j: pl.BlockSpec((None, length, E_HY), lambda b, f: (b, 0, j))
    return pl.pallas_call(
        functools.partial(_hyena_kernel, length=length),
        out_shape=jax.ShapeDtypeStruct((nb, length, E_HY), BF16),
        grid=(nb, length // tf),
        in_specs=[
            col(COL_HY_V), col(COL_HY_X0), col(COL_HY_X1),
            pl.BlockSpec((None, 3, 3 * E_HY), lambda b, f: (layer, 0, 0)),
            pl.BlockSpec((None, 1, 3 * E_HY), lambda b, f: (layer, 0, 0)),
            pl.BlockSpec((None, 1, E_HY), lambda b, f: (layer, 0, 0)),
            pl.BlockSpec((SUBLANES, length), lambda b, f: (0, 0)),
            pl.BlockSpec((SUBLANES, E_HY), lambda b, f: (0, 0)),
            pl.BlockSpec((tf, length), lambda b, f: (f, 0)),
            pl.BlockSpec((tf, length), lambda b, f: (f, 0)),
            pl.BlockSpec((length, tf), lambda b, f: (0, f)),
            pl.BlockSpec((length, tf), lambda b, f: (0, f)),
            pl.BlockSpec((tf, E_HY), lambda b, f: (f, 0)),
            pl.BlockSpec((tf, E_HY), lambda b, f: (f, 0)),
        ],
        out_specs=pl.BlockSpec((None, length, E_HY), lambda b, f: (b, 0, 0)),
        scratch_shapes=[pltpu.VMEM((length, E_HY), BF16), pltpu.VMEM((length, E_HY), BF16),
                        pltpu.VMEM((length, E_HY), F32), pltpu.VMEM((SUBLANES, E_HY), F32)],
        compiler_params=_cparams("parallel", "arbitrary"),
        name="hyena",
    )(p, p, p, short_w, short_b, skip, sign_rows, kn, cos_t, sin_t, cos_t, sin_t, kr, ki)


LRU_PAD = SUBLANES
LRU_CHUNK = 128


def _lru_kernel(xc_ref, gc_ref, xl_ref, gl_ref, cw_ref, cb_ref, wg_ref, bg_ref, lam_ref, yc_ref, yl_ref,
                xpad, a_cum, b_cum, h_sum, *, tc, tl):
    off_c = LRU_PAD
    off_l = off_c + tc + LRU_PAD
    n = LRU_CHUNK
    nj = E_LRU // LANES
    n_chunks_c = tc // n
    n_chunks = (tc + tl) // n
    groups_c = tc // SUBLANES
    groups = (tc + tl) // SUBLANES

    xpad[...] = jnp.zeros_like(xpad)
    xpad[off_c:off_c + tc, :] = xc_ref[...].astype(F32)
    xpad[off_l:off_l + tl, :] = xl_ref[...].astype(F32)

    sub = lax.broadcasted_iota(jnp.int32, (n // SUBLANES, SUBLANES, LANES), 1)

    for d in range(2):
        lam = lam_ref[d:d + 1, :]
        softplus_neg = jnp.maximum(-lam, 0.0) + jnp.log1p(jnp.exp(-jnp.abs(lam)))

        def phase_a(k, carry, d=d, softplus_neg=softplus_neg):
            start = pl.multiple_of(off_c + k * n + jnp.where(k >= n_chunks_c, LRU_PAD, 0), SUBLANES)
            for j in range(nj):
                cols = slice(j * LANES, (j + 1) * LANES)
                win = xpad[pl.ds(start - LRU_PAD, n + 2 * LRU_PAD), cols]
                xc = cb_ref[d:d + 1, cols] + jnp.zeros((n, LANES), F32)
                for tap in range(LRU_CONV):
                    back = LRU_CONV - 1 - tap
                    shift = back if d == 0 else (n + 2 * LRU_PAD - back) % (n + 2 * LRU_PAD)
                    src = win if back == 0 else pltpu.roll(win, shift, 0)
                    xc = xc + cw_ref[d, tap:tap + 1, cols] * src[LRU_PAD:LRU_PAD + n]
                gates = _dot(xc.astype(BF16), wg_ref[d, j]) + bg_ref[d, j:j + 1, :]
                r = jax.nn.sigmoid(gates[:, :LANES])
                i = jax.nn.sigmoid(gates[:, LANES:])
                log_a = (-LRU_C) * r * softplus_neg[:, cols]
                a = jnp.exp(log_a)
                b = jnp.sqrt(-jnp.tanh(log_a) * (a * a + 1.0)) * (i * xc)
                a = a.reshape(n // SUBLANES, SUBLANES, LANES)
                b = b.reshape(n // SUBLANES, SUBLANES, LANES)
                for s in (1, 2, 4):
                    if d == 0:
                        keep = sub >= s
                        a_prev = pltpu.roll(a, s, 1)
                        b_prev = pltpu.roll(b, s, 1)
                    else:
                        keep = sub < SUBLANES - s
                        a_prev = pltpu.roll(a, SUBLANES - s, 1)
                        b_prev = pltpu.roll(b, SUBLANES - s, 1)
                    b = jnp.where(keep, a * b_prev + b, b)
                    a = jnp.where(keep, a * a_prev, a)
                a_cum[pl.ds(start, n), cols] = a.reshape(n, LANES)
                b_cum[pl.ds(start, n), cols] = b.reshape(n, LANES)
            return carry

        lax.fori_loop(0, n_chunks, phase_a, 0)

        def phase_b(g, h_prev, d=d):
            if d == 0:
                row = off_c + g * SUBLANES + jnp.where(g >= groups_c, LRU_PAD, 0)
            else:
                row = jnp.where(g < groups_c, off_c + (groups_c - 1 - g) * SUBLANES,
                                off_l + (groups - 1 - g) * SUBLANES)
            rows = pl.ds(pl.multiple_of(row, SUBLANES), SUBLANES)
            h = a_cum[rows, :] * h_prev + b_cum[rows, :]
            if d == 0:
                h_sum[rows, :] = h
                return jnp.broadcast_to(h[SUBLANES - 1:SUBLANES, :], (SUBLANES, E_LRU))
            h_sum[rows, :] = h_sum[rows, :] + h
            return jnp.broadcast_to(h[0:1, :], (SUBLANES, E_LRU))

        lax.fori_loop(0, groups, phase_b, jnp.zeros((SUBLANES, E_LRU), F32), unroll=4)

    yc_ref[...] = (h_sum[off_c:off_c + tc, :] * _gelu_tanh(gc_ref[...].astype(F32))).astype(BF16)

    def phase_c(k, carry):
        r0 = pl.multiple_of(k * 256, 256)
        yl_ref[pl.ds(r0, 256), :] = (h_sum[pl.ds(off_l + r0, 256), :]
                                     * _gelu_tanh(gl_ref[pl.ds(r0, 256), :].astype(F32))).astype(BF16)
        return carry

    lax.fori_loop(0, tl // 256, phase_c, 0)


def _lru_gate_weights(wr, wi, br, bi):
    per = LANES // LRU_BLOCK_DIM
    nj = E_LRU // LANES

    def lane_blocks(w):
        w = w.reshape(DEPTH, 2, nj, per, LRU_BLOCK_DIM, LRU_BLOCK_DIM)
        eye = jnp.eye(per, dtype=w.dtype)
        full = w[:, :, :, :, :, None, :] * eye[None, None, None, :, None, :, None]
        return full.reshape(DEPTH, 2, nj, LANES, LANES)

    wg = jnp.concatenate([lane_blocks(wr), lane_blocks(wi)], axis=-1).astype(BF16)
    bg = jnp.concatenate([br.reshape(DEPTH, 2, nj, LANES), bi.reshape(DEPTH, 2, nj, LANES)], axis=-1)
    return wg, bg


def _lru(p_ctx, p_lat, layer, conv_w, conv_b, wg, bg, lam):
    nb, tc, _ = p_ctx.shape
    tl = p_lat.shape[1]
    rows = tc + tl + 3 * LRU_PAD
    col = lambda t, j: pl.BlockSpec((None, t, E_LRU), lambda b: (b, 0, j))
    par = lambda shape: pl.BlockSpec((None,) + shape, lambda b: (layer,) + (0,) * len(shape))
    return pl.pallas_call(
        functools.partial(_lru_kernel, tc=tc, tl=tl),
        out_shape=[jax.ShapeDtypeStruct((nb, tc, E_LRU), BF16), jax.ShapeDtypeStruct((nb, tl, E_LRU), BF16)],
        grid=(nb,),
        in_specs=[col(tc, COL_LRU_X), col(tc, COL_LRU_G), col(tl, COL_LRU_X), col(tl, COL_LRU_G),
                  par((2, LRU_CONV, E_LRU)), par((2, E_LRU)), par((2, E_LRU // LANES, LANES, 2 * LANES)),
                  par((2, E_LRU // LANES, 2 * LANES)), par((2, E_LRU))],
        out_specs=[pl.BlockSpec((None, tc, E_LRU), lambda b: (b, 0, 0)),
                   pl.BlockSpec((None, tl, E_LRU), lambda b: (b, 0, 0))],
        scratch_shapes=[pltpu.VMEM((rows, E_LRU), F32)] * 4,
        compiler_params=_cparams("parallel"),
        name="lru",
    )(p_ctx, p_ctx, p_lat, p_lat, conv_w, conv_b, wg, bg, lam)


def _cumsum_chunk(x, sub, reverse):
    groups = x.shape[0] // SUBLANES
    x3 = x.reshape(groups, SUBLANES, x.shape[1])
    for s in (1, 2, 4):
        if reverse:
            x3 = x3 + jnp.where(sub < SUBLANES - s, pltpu.roll(x3, SUBLANES - s, 1), 0.0)
        else:
            x3 = x3 + jnp.where(sub >= s, pltpu.roll(x3, s, 1), 0.0)
    edge = 0 if reverse else SUBLANES - 1
    blocks = [None] * groups
    total = None
    for g in (range(groups - 1, -1, -1) if reverse else range(groups)):
        blk = x3[g] if total is None else x3[g] + total
        blocks[g] = blk
        total = blk[edge:edge + 1, :]
    return blocks


def _hgrn_kernel(qc_ref, ffc_ref, fbc_ref, ic_ref, ogc_ref, ql_ref, ffl_ref, fbl_ref, il_ref, ogl_ref,
                 lb_ref, ng_ref, yc_ref, yl_ref, o_acc, o_f, o_b, qg_s, kd_s, v_s, dec_s, *, layer, tc, tl):
    ch = HG_CHUNK
    raw = [lb_ref[i:i + 1, :] for i in range(DEPTH)]
    top = functools.reduce(jnp.maximum, raw)
    ex = [jnp.exp(r - top) for r in raw]
    tot = functools.reduce(lambda a, b: a + b, ex)
    lb = jnp.zeros((1, E_HG), F32)
    for i in range(1, layer + 1):
        lb = lb + ex[i] / tot

    rt = lax.broadcasted_iota(jnp.int32, (ch, ch), 0)
    ct = lax.broadcasted_iota(jnp.int32, (ch, ch), 1)
    tri = [(rt >= ct), (rt <= ct)]
    sub = lax.broadcasted_iota(jnp.int32, (ch // SUBLANES, SUBLANES, HG_DK), 1)
    n_c, n_l = tc // ch, tl // ch
    n_all = n_c + n_l
    mid_g = (ch // 2) // SUBLANES
    last = SUBLANES - 1

    def intra(q_ref, ff_ref, fb_ref, i_ref, n_chunks, chunk0):
        def body(c, carry):
            rows = pl.ds(pl.multiple_of(c * ch, ch), ch)
            srows = pl.ds(pl.multiple_of((chunk0 + c) * ch, ch), ch)
            drows = pl.ds(pl.multiple_of((chunk0 + c) * SUBLANES, SUBLANES), SUBLANES)
            for hd in range(HG_HEADS):
                cols = slice(hd * HG_DK, (hd + 1) * HG_DK)
                lbh = lb[:, cols]
                q = _silu(q_ref[rows, cols].astype(F32))
                v = i_ref[rows, cols]
                v_s[srows, cols] = v
                o = None
                for d, f_ref in enumerate((ff_ref, fb_ref)):
                    f = lbh + (1.0 - lbh) * jax.nn.sigmoid(f_ref[rows, cols].astype(F32))
                    k = 1.0 - f
                    blocks = _cumsum_chunk(jnp.log(f), sub, reverse=d == 1)
                    b = jnp.concatenate(blocks, axis=0)
                    if d == 0:
                        b_last, b_mid = blocks[-1][last:last + 1], blocks[mid_g][0:1]
                    else:
                        b_last, b_mid = blocks[0][0:1], blocks[mid_g - 1][last:last + 1]
                    e_mid = jnp.exp(b - b_mid)
                    qs = (q * e_mid).astype(BF16)
                    ks = (k * jnp.exp(b_mid - b)).astype(BF16)
                    qg_s[d, srows, cols] = (q * e_mid * jnp.exp(b_mid)).astype(BF16)
                    kd_s[d, srows, cols] = (k * jnp.exp(b_last - b)).astype(BF16)
                    dec_s[d, drows, cols] = jnp.broadcast_to(jnp.exp(b_last), (SUBLANES, HG_DK))
                    scores = jnp.where(tri[d], _dot_nt(qs, ks), 0.0)
                    od = _dot(scores.astype(BF16), v)
                    o = od if o is None else o + od
                o_acc[srows, cols] = o
            return carry

        lax.fori_loop(0, n_chunks, body, 0, unroll=4)

    intra(qc_ref, ffc_ref, fbc_ref, ic_ref, n_c, 0)
    intra(ql_ref, ffl_ref, fbl_ref, il_ref, n_l, n_c)

    def inter(it, states):
        new = []
        for d in range(2):
            g = it if d == 0 else jnp.where(it < n_c, n_c - 1 - it, n_all + n_c - 1 - it)
            rows = pl.ds(pl.multiple_of(g * ch, ch), ch)
            drows = pl.ds(pl.multiple_of(g * SUBLANES, SUBLANES), SUBLANES)
            o_d = o_f if d == 0 else o_b
            for hd in range(HG_HEADS):
                cols = slice(hd * HG_DK, (hd + 1) * HG_DK)
                s_t = states[d * HG_HEADS + hd]
                o_d[rows, cols] = _dot_nt(qg_s[d, rows, cols], s_t.astype(BF16))
                dec = dec_s[d, drows, cols][0:1]
                new.append(s_t * dec + _dot_tn(v_s[rows, cols], kd_s[d, rows, cols]))
        return tuple(new)

    lax.fori_loop(0, n_all, inter, tuple(jnp.zeros((HG_DV, HG_DK), F32) for _ in range(2 * HG_HEADS)), unroll=4)

    def finish(og_ref, y_ref, base, t):
        blk = min(t, 256)

        def body(k, carry):
            r0 = pl.multiple_of(k * blk, blk)
            for hd in range(HG_HEADS):
                cols = slice(hd * HG_DV, (hd + 1) * HG_DV)
                orow = pl.ds(base + r0, blk)
                o = o_acc[orow, cols] + o_f[orow, cols] + o_b[orow, cols]
                y = _rms(o) * ng_ref[...] * _silu(og_ref[pl.ds(r0, blk), cols].astype(F32))
                y_ref[pl.ds(r0, blk), cols] = y.astype(BF16)
            return carry

        lax.fori_loop(0, t // blk, body, 0)

    finish(ogc_ref, yc_ref, 0, tc)
    finish(ogl_ref, yl_ref, tc, tl)


def _hgrn(p_ctx, p_lat, layer, lower_bounds, norm_g):
    nb, tc, _ = p_ctx.shape
    tl = p_lat.shape[1]
    t = tc + tl
    col = lambda n, j: pl.BlockSpec((None, n, E_HG), lambda b: (b, 0, j), pipeline_mode=pl.Buffered(1))
    names = (COL_HG_Q, COL_HG_FF, COL_HG_FB, COL_HG_I, COL_HG_OG)
    return pl.pallas_call(
        functools.partial(_hgrn_kernel, layer=layer, tc=tc, tl=tl),
        out_shape=[jax.ShapeDtypeStruct((nb, tc, E_HG), BF16), jax.ShapeDtypeStruct((nb, tl, E_HG), BF16)],
        grid=(nb,),
        in_specs=[col(tc, j) for j in names] + [col(tl, j) for j in names] + [
            pl.BlockSpec((DEPTH, E_HG), lambda b: (0, 0)),
            pl.BlockSpec((None, 1, HG_DV), lambda b: (layer, 0, 0)),
        ],
        out_specs=[pl.BlockSpec((None, tc, E_HG), lambda b: (b, 0, 0)),
                   pl.BlockSpec((None, tl, E_HG), lambda b: (b, 0, 0))],
        scratch_shapes=[pltpu.VMEM((t, E_HG), F32), pltpu.VMEM((t, E_HG), F32), pltpu.VMEM((t, E_HG), F32),
                        pltpu.VMEM((2, t, E_HG), BF16), pltpu.VMEM((2, t, E_HG), BF16), pltpu.VMEM((t, E_HG), BF16),
                        pltpu.VMEM((2, t // HG_CHUNK * SUBLANES, E_HG), F32)],
        compiler_params=_cparams("parallel"),
        name="hgrn",
    )(*([p_ctx] * 5 + [p_lat] * 5 + [lower_bounds, norm_g]))


def _merge_kernel(gh_ref, gl_ref, gg_ref, yh_ref, yl_ref, yg_ref, wh_ref, wl_ref, wg_ref, wo_ref,
                  h_ref, g1_ref, gain_ref, o_ref, *t_ref, n_cols, n_rows):
    def branch(g_ref, y_ref, w_ref):
        return jax.nn.sigmoid(g_ref[...].astype(F32)) * _dot(y_ref[...], w_ref[...])

    m = branch(gh_ref, yh_ref, wh_ref) + branch(gl_ref, yl_ref, wl_ref) + branch(gg_ref, yg_ref, wg_ref)
    r = _rms(_dot(m.astype(BF16), wo_ref[...])) * (gain_ref[...] * g1_ref[...])
    if n_cols:
        for j in range(D_MODEL // LANES):
            lanes = slice(j * LANES, (j + 1) * LANES)
            for c in range(n_cols):
                t_ref[0][j, pl.ds(c, n_rows, stride=n_cols), :] = r[c * n_rows:(c + 1) * n_rows, lanes]
            o_ref[:, :, lanes] = h_ref[:, :, lanes] + t_ref[0][j].reshape(n_rows, n_cols, LANES)
    else:
        o_ref[...] = h_ref[...] + r


def _merge(p, y_hy, y_lru, y_hg, w_hy, w_lru, w_hg, w_out, h, ada4, gains4, layer, *, ctx_row, col_major):
    nb, t, _ = h.shape
    tm = 512
    row_fn = (lambda b: b) if ctx_row is None else (lambda b: ctx_row)
    if col_major:
        n_rows = t // GRID_W
        n_cols = tm // n_rows
        h = h.reshape(nb, n_rows, GRID_W, D_MODEL)
        h_spec = pl.BlockSpec((None, n_rows, n_cols, D_MODEL), lambda b, i: (b, 0, i, 0))
        scratch = [pltpu.VMEM((D_MODEL // LANES, tm, LANES), F32)]
    else:
        n_rows = n_cols = 0
        h_spec = pl.BlockSpec((None, tm, D_MODEL), lambda b, i: (b, i, 0))
        scratch = []
    gate = lambda k: pl.BlockSpec((None, tm, D_MODEL), lambda b, i: (b, i, COL_GATES + k))
    y_spec = pl.BlockSpec((None, tm, E_HY), lambda b, i: (b, i, 0))
    w_spec = pl.BlockSpec((None, E_HY, D_MODEL), lambda b, i: (layer, 0, 0))
    out = pl.pallas_call(
        functools.partial(_merge_kernel, n_cols=n_cols, n_rows=n_rows),
        out_shape=jax.ShapeDtypeStruct(h.shape, F32),
        grid=(nb, t // tm),
        in_specs=[gate(0), gate(1), gate(2), y_spec, y_spec, y_spec, w_spec, w_spec, w_spec,
                  pl.BlockSpec((None, D_MODEL, D_MODEL), lambda b, i: (layer, 0, 0)),
                  h_spec, _ada_spec(layer, row_fn, 2, 2), _gain_spec(layer, 1, 2)],
        out_specs=h_spec,
        scratch_shapes=scratch,
        compiler_params=_cparams("parallel", "parallel"),
        name="merge",
    )(p, p, p, y_hy, y_lru, y_hg, w_hy, w_lru, w_hg, w_out, h, ada4, gains4)
    return out.reshape(nb, t, D_MODEL)


def _mlp_kernel(h_ref, sh_ref, sc_ref, g2_ref, gain2_ref, gain3_ref, w1_ref, w2_ref, o_ref):
    h = h_ref[...]
    u = (_rms(h) * (gain2_ref[...] * (1.0 + sc_ref[...])) + sh_ref[...]).astype(BF16)
    acc = None
    for k in range(0, D_FF, MLP_FF_BLOCK):
        a = jnp.maximum(_dot(u, w1_ref[:, k:k + MLP_FF_BLOCK]), 0.0)
        part = _dot((a * a).astype(BF16), w2_ref[k:k + MLP_FF_BLOCK, :])
        acc = part if acc is None else acc + part
    o_ref[...] = h + (g2_ref[...] * gain3_ref[...]) * _rms(acc)


def _mlp(h, ada4, gains4, w1_b, w2_b, layer, *, ctx_row):
    nb, t, _ = h.shape
    tm = 512
    row_fn = (lambda b: b) if ctx_row is None else (lambda b: ctx_row)
    h_spec = pl.BlockSpec((None, tm, D_MODEL), lambda b, i: (b, i, 0))
    return pl.pallas_call(
        _mlp_kernel,
        out_shape=jax.ShapeDtypeStruct(h.shape, F32),
        grid=(nb, t // tm),
        in_specs=[h_spec, _ada_spec(layer, row_fn, 3, 2), _ada_spec(layer, row_fn, 4, 2),
                  _ada_spec(layer, row_fn, 5, 2), _gain_spec(layer, 2, 2), _gain_spec(layer, 3, 2),
                  pl.BlockSpec((None, D_MODEL, D_FF), lambda b, i: (layer, 0, 0), pipeline_mode=pl.Buffered(1)),
                  pl.BlockSpec((None, D_FF, D_MODEL), lambda b, i: (layer, 0, 0), pipeline_mode=pl.Buffered(1))],
        out_specs=h_spec,
        compiler_params=_cparams("parallel", "parallel"),
        name="mlp",
    )(h, ada4, ada4, ada4, gains4, gains4, w1_b, w2_b)


def kernel(x, c, ctx, c_ctx, w_ada, b_ada, norm_gains, w_in, hy_short_w, hy_short_b, hy_ff_w1, hy_ff_b1,
           hy_ff_w2, hy_ff_b2, hy_ff_w3, hy_freq, hy_skip, lru_conv_w, lru_conv_b, lru_wr, lru_br, lru_wi,
           lru_bi, lru_lambda, hg_lower_bounds, hg_norm_g, w_proj_hy, w_proj_lru, w_proj_hg, w_out,
           w_mlp1, w_mlp2):
    nb, seq, _ = x.shape
    ctx_len = ctx.shape[1]
    assert seq % GRID_W == 0

    w_in_b, w_out_b = w_in.astype(BF16), w_out.astype(BF16)
    w_hy_b, w_lru_b, w_hg_b = w_proj_hy.astype(BF16), w_proj_lru.astype(BF16), w_proj_hg.astype(BF16)
    w1_b, w2_b = w_mlp1.astype(BF16), w_mlp2.astype(BF16)

    ctx_row = nb
    ada_rows = -(-(nb + 1) // SUBLANES) * SUBLANES
    cin = jnp.concatenate([c, c_ctx[None], jnp.zeros((ada_rows - nb - 1, D_MODEL), F32)], axis=0)
    ada4 = _ada_call(cin, w_ada.astype(BF16), b_ada).reshape(DEPTH, ada_rows, 1, N_ADA * D_MODEL)
    gains4 = norm_gains.reshape(DEPTH, 4, 1, D_MODEL)

    hy_w1p = jnp.pad(hy_ff_w1, ((0, 0), (0, LANES - hy_ff_w1.shape[1]), (0, 0)))
    hy_b1 = hy_ff_b1.reshape(DEPTH, 1, HY_ORDER)
    hy_b2 = hy_ff_b2.reshape(DEPTH, 1, HY_ORDER)
    hy_sb = hy_short_b.reshape(DEPTH, 1, 3 * E_HY)
    hy_sk = hy_skip.reshape(DEPTH, 1, E_HY)
    lru_wg, lru_bg = _lru_gate_weights(lru_wr, lru_wi, lru_br, lru_bi)
    hg_ng = hg_norm_g.reshape(DEPTH, 1, HG_DV)

    tables = {seq: _dft_tables(seq), ctx_len: _dft_tables(ctx_len)}

    def hyena(p, layer):
        length = p.shape[1]
        hs, hd = _hy_taps(length, layer, hy_w1p, hy_b1, hy_ff_w2, hy_b2, hy_ff_w3, hy_freq)
        spectrum = _hy_spectrum(length, *tables[length], hs, hd)
        return _hyena(p, layer, hy_short_w, hy_sb, hy_sk, tables[length], spectrum)

    h_lat = x
    h_ctx = ctx.reshape(1, nb * ctx_len, D_MODEL)
    for layer in range(DEPTH):
        need_ctx = layer < DEPTH - 1
        col_major = layer % 2 == 1
        p_lat = _in_proj(h_lat, ada4, gains4, w_in_b, layer, ctx_row=None, col_major=col_major)
        p_ctx_flat = _in_proj(h_ctx, ada4, gains4, w_in_b, layer, ctx_row=ctx_row, col_major=False)
        p_ctx = p_ctx_flat.reshape(nb, ctx_len, IN_WIDTH)

        y_hy_lat = hyena(p_lat, layer)
        y_lru_ctx, y_lru_lat = _lru(p_ctx, p_lat, layer, lru_conv_w, lru_conv_b, lru_wg, lru_bg, lru_lambda)
        y_hg_ctx, y_hg_lat = _hgrn(p_ctx, p_lat, layer, hg_lower_bounds, hg_ng)
        proj = (w_hy_b, w_lru_b, w_hg_b, w_out_b)
        h_lat = _merge(p_lat, y_hy_lat, y_lru_lat, y_hg_lat, *proj, h_lat, ada4, gains4, layer,
                       ctx_row=None, col_major=col_major)
        if need_ctx:
            flat = lambda y: y.reshape(1, nb * ctx_len, y.shape[-1])
            y_hy_ctx = hyena(p_ctx, layer)
            h_ctx = _merge(p_ctx_flat, flat(y_hy_ctx), flat(y_lru_ctx), flat(y_hg_ctx), *proj, h_ctx, ada4,
                           gains4, layer, ctx_row=ctx_row, col_major=False)

        h_lat = _mlp(h_lat, ada4, gains4, w1_b, w2_b, layer, ctx_row=None)
        if need_ctx:
            h_ctx = _mlp(h_ctx, ada4, gains4, w1_b, w2_b, layer, ctx_row=ctx_row)
    return h_lat
```

```python
import functools
import math

import jax
import jax.numpy as jnp
from jax import lax
from jax.experimental import pallas as pl
from jax.experimental.pallas import tpu as pltpu

F32 = jnp.float32
BF16 = jnp.bfloat16
HIGHEST = lax.Precision.HIGHEST

D_MODEL = 1024
DEPTH = 2
GRID_W = 64
EPS = 1e-6

E_HY = 512
HY_EMB_BANDS = 8
HY_ORDER = 64
HY_DECAY_TARGET = 1e-2
HY_FAST_DECAY = 0.3
HY_SLOW_DECAY = 1.5
HY_ROWS = 64
DFT_COARSE = 64

E_LRU = 512
LRU_BLOCKS = 8
LRU_BLOCK_DIM = E_LRU // LRU_BLOCKS
LRU_CONV = 4
LRU_C = 8.0

HG_HEADS = 4
HG_DK = 128
HG_DV = 128
E_HG = HG_HEADS * HG_DV
HG_CHUNK = 64
HG_TRIP = 4

D_FF = 4 * D_MODEL
IN_WIDTH = 8 * D_MODEL
N_ADA = 6

COL_HY_V, COL_HY_X0, COL_HY_X1 = 0, 1, 2
COL_LRU_X, COL_LRU_G = 3, 4
COL_HG_Q, COL_HG_FF, COL_HG_FB, COL_HG_I, COL_HG_OG = 5, 6, 7, 8, 9
COL_GATES = 5

LANES = 128
SUBLANES = 8
VMEM_LIMIT = 56 * 1024 * 1024
NORM_ROWS = 512
MLP_FF_BLOCK = 1024


def _cparams(*sem):
    return pltpu.CompilerParams(dimension_semantics=sem, vmem_limit_bytes=VMEM_LIMIT)


def _rms(x):
    return x * lax.rsqrt(jnp.mean(x * x, axis=-1, keepdims=True) + EPS)


def _silu(x):
    return x * jax.nn.sigmoid(x)


def _gelu_tanh(x):
    return x * (0.5 * (1.0 + jnp.tanh(math.sqrt(2.0 / math.pi) * (x + 0.044715 * (x * x * x)))))


def _dot(a, b):
    return jnp.dot(a, b, preferred_element_type=F32)


def _dot_nt(a, b):
    return lax.dot_general(a, b, (((1,), (1,)), ((), ())), preferred_element_type=F32)


def _dot_tn(a, b):
    return lax.dot_general(a, b, (((0,), (0,)), ((), ())), preferred_element_type=F32)


def _ada_spec(layer, row_fn, chunk, nargs):
    if nargs == 2:
        return pl.BlockSpec((None, None, 1, D_MODEL), lambda b, i: (layer, row_fn(b), 0, chunk))
    return pl.BlockSpec((None, None, 1, D_MODEL), lambda b, i, j: (layer, row_fn(b), 0, chunk))


def _gain_spec(layer, idx, nargs):
    if nargs == 2:
        return pl.BlockSpec((None, None, 1, D_MODEL), lambda b, i: (layer, idx, 0, 0))
    return pl.BlockSpec((None, None, 1, D_MODEL), lambda b, i, j: (layer, idx, 0, 0))


def _ada_kernel(c_ref, w_ref, b_ref, o_ref):
    s = _silu(c_ref[...])
    o_ref[...] = _dot(s.astype(BF16), w_ref[...].astype(BF16)) + b_ref[...]


def _ada_call(cin, w_ada_b, b_ada):
    tn = D_MODEL
    rows = cin.shape[0]
    return pl.pallas_call(
        _ada_kernel,
        out_shape=jax.ShapeDtypeStruct((DEPTH, rows, N_ADA * D_MODEL), F32),
        grid=(DEPTH, N_ADA * D_MODEL // tn),
        in_specs=[
            pl.BlockSpec((rows, D_MODEL), lambda l, j: (0, 0)),
            pl.BlockSpec((None, D_MODEL, tn), lambda l, j: (l, 0, j)),
            pl.BlockSpec((None, 1, tn), lambda l, j: (l, 0, j)),
        ],
        out_specs=pl.BlockSpec((None, rows, tn), lambda l, j: (l, 0, j)),
        compiler_params=_cparams("parallel", "parallel"),
        name="ada",
    )(cin, w_ada_b, b_ada.reshape(DEPTH, 1, N_ADA * D_MODEL))


def _inproj_kernel(h_ref, sh_ref, sc_ref, g_ref, w_ref, o_ref, u_ref, *t_ref, n_cols, n_rows):
    @pl.when(pl.program_id(2) == 0)
    def _():
        mul = g_ref[...] * (1.0 + sc_ref[...])
        sh = sh_ref[...]
        if n_cols:
            def grid_row(r, carry):
                y = _rms(h_ref[r]) * mul + sh
                rows = pl.ds(pl.multiple_of(r * n_cols, n_cols), n_cols)
                for j in range(D_MODEL // LANES):
                    t_ref[0][j, rows, :] = y[:, j * LANES:(j + 1) * LANES]
                return carry

            lax.fori_loop(0, n_rows, grid_row, 0, unroll=4)
            for j in range(D_MODEL // LANES):
                lanes = slice(j * LANES, (j + 1) * LANES)
                for c in range(n_cols):
                    u_ref[c * n_rows:(c + 1) * n_rows, lanes] = (
                        t_ref[0][j, pl.ds(c, n_rows, stride=n_cols), :].astype(BF16))
        else:
            for k in range(0, u_ref.shape[0], NORM_ROWS):
                u_ref[k:k + NORM_ROWS, :] = (_rms(h_ref[k:k + NORM_ROWS, :]) * mul + sh).astype(BF16)

    o_ref[...] = _dot(u_ref[...], w_ref[...]).astype(BF16)


def _in_proj(h, ada4, gains4, w_in_b, layer, *, ctx_row, col_major):
    nb, t, _ = h.shape
    tm, tn = min(2048, t), 1024
    row_fn = (lambda b: b) if ctx_row is None else (lambda b: ctx_row)
    if col_major:
        n_rows = t // GRID_W
        n_cols = tm // n_rows
        h = h.reshape(nb, n_rows, GRID_W, D_MODEL)
        h_spec = pl.BlockSpec((None, n_rows, n_cols, D_MODEL), lambda b, i, j: (b, 0, i, 0))
        scratch = [pltpu.VMEM((tm, D_MODEL), BF16), pltpu.VMEM((D_MODEL // LANES, tm, LANES), F32)]
    else:
        n_rows = n_cols = 0
        h_spec = pl.BlockSpec((None, tm, D_MODEL), lambda b, i, j: (b, i, 0))
        scratch = [pltpu.VMEM((tm, D_MODEL), BF16)]
    return pl.pallas_call(
        functools.partial(_inproj_kernel, n_cols=n_cols, n_rows=n_rows),
        out_shape=jax.ShapeDtypeStruct((nb, t, IN_WIDTH), BF16),
        grid=(nb, t // tm, IN_WIDTH // tn),
        in_specs=[
            h_spec,
            _ada_spec(layer, row_fn, 0, 3),
            _ada_spec(layer, row_fn, 1, 3),
            _gain_spec(layer, 0, 3),
            pl.BlockSpec((None, D_MODEL, tn), lambda b, i, j: (layer, 0, j)),
        ],
        out_specs=pl.BlockSpec((None, tm, tn), lambda b, i, j: (b, i, j)),
        scratch_shapes=scratch,
        compiler_params=_cparams("parallel", "parallel", "arbitrary"),
        name="in_proj",
    )(h, ada4, ada4, gains4, w_in_b)


def _dft_tables(length):
    n = 2 * length
    idx = jnp.arange(length, dtype=jnp.int32)

    def trig(freqs):
        ang = ((freqs[:, None] * idx[None, :]) % n).astype(F32) * (2.0 * math.pi / n)
        return jnp.cos(ang), jnp.sin(ang)

    c_hi, s_hi = trig(jnp.arange(length // DFT_COARSE, dtype=jnp.int32) * DFT_COARSE)
    c_lo, s_lo = trig(jnp.arange(DFT_COARSE, dtype=jnp.int32))
    cos_t = (c_hi[:, None] * c_lo[None] - s_hi[:, None] * s_lo[None]).reshape(length, length)
    sin_t = (s_hi[:, None] * c_lo[None] + c_hi[:, None] * s_lo[None]).reshape(length, length)
    sign = (1 - 2 * (idx % 2)).astype(F32)
    sign_rows = jnp.zeros((SUBLANES, length), F32).at[0].set(sign)
    return cos_t.astype(BF16), sin_t.astype(BF16), sign_rows.astype(BF16)


def _hy_taps_kernel(w1_ref, b1_ref, w2_ref, b2_ref, w3_ref, fr_ref, hs_ref, hd_ref, *, length, tl):
    row0 = pl.program_id(0) * tl
    m = lax.broadcasted_iota(jnp.int32, (tl, LANES), 0) + row0
    m_wide = lax.broadcasted_iota(jnp.int32, (tl, E_HY), 0) + row0
    lane = lax.broadcasted_iota(jnp.int32, (tl, LANES), 1)
    band_step = (HY_EMB_BANDS - 1 - 1e-4) / (HY_EMB_BANDS - 1)
    band = 1e-4 + jnp.where(lane <= HY_EMB_BANDS, lane - 1, lane - 1 - HY_EMB_BANDS).astype(F32) * band_step
    ch = lax.broadcasted_iota(jnp.int32, (1, E_HY), 1).astype(F32)
    lo = math.log(HY_DECAY_TARGET) / HY_SLOW_DECAY
    hi = math.log(HY_DECAY_TARGET) / HY_FAST_DECAY
    delta = jnp.abs(lo + ch * ((hi - lo) / (E_HY - 1)))

    def taps(offset):
        pos = (2 * m + offset).astype(F32)
        t = pos / max(length - 1, 1)
        ang = band * (2.0 * math.pi * pos / length)
        z = jnp.where(lane == 0, t,
                      jnp.where(lane <= HY_EMB_BANDS, jnp.cos(ang),
                                jnp.where(lane <= 2 * HY_EMB_BANDS, -jnp.sin(ang), 0.0)))
        h = jnp.sin(fr_ref[0:1, :] * (jnp.dot(z, w1_ref[...], precision=HIGHEST, preferred_element_type=F32)
                                      + b1_ref[...]))
        h = jnp.sin(fr_ref[1:2, :] * (jnp.dot(h, w2_ref[...], precision=HIGHEST, preferred_element_type=F32)
                                      + b2_ref[...]))
        h = jnp.dot(h, w3_ref[...], precision=HIGHEST, preferred_element_type=F32)
        decay = jnp.exp(-((2 * m_wide + offset).astype(F32) / max(length - 1, 1)) * delta)
        return h[:, :E_HY] * decay, h[:, E_HY:] * decay

    f_even, b_even = taps(0)
    f_odd, b_odd = taps(1)
    f_prev, b_prev = taps(-1)
    first = m_wide == 0
    forward = (f_even, f_odd, jnp.where(first, b_odd, f_prev))
    backward = (b_even, b_prev, b_odd)
    for x in range(3):
        cols = slice(x * E_HY, (x + 1) * E_HY)
        bwd = jnp.where(first, 0.0, backward[x])
        hs_ref[:, cols] = (forward[x] + bwd).astype(BF16)
        hd_ref[:, cols] = (bwd - forward[x]).astype(BF16)


def _hy_taps(length, layer, w1p, b1, w2, b2, w3, freq):
    half = length // 2
    tl = min(256, half)
    full = lambda shape: pl.BlockSpec((None,) + shape, lambda i: (layer,) + (0,) * len(shape))
    return pl.pallas_call(
        functools.partial(_hy_taps_kernel, length=length, tl=tl),
        out_shape=[jax.ShapeDtypeStruct((half, 3 * E_HY), BF16)] * 2,
        grid=(half // tl,),
        in_specs=[full((LANES, HY_ORDER)), full((1, HY_ORDER)), full((HY_ORDER, HY_ORDER)), full((1, HY_ORDER)),
                  full((HY_ORDER, 2 * E_HY)), full((2, HY_ORDER))],
        out_specs=[pl.BlockSpec((tl, 3 * E_HY), lambda i: (i, 0))] * 2,
        compiler_params=_cparams("parallel"),
        name="hy_taps",
    )(w1p, b1, w2, b2, w3, freq)


def _hy_spec_kernel(c_ref, s_ref, sg_ref, hs_ref, hd_ref, kr_ref, ki_ref, kn_ref, *, length, tf):
    n = 2 * length
    row = lax.broadcasted_iota(jnp.int32, kr_ref.shape, 0) + pl.program_id(0) * tf
    scale = jnp.where(row == 0, 1.0 / n, 2.0 / n)
    kr_ref[...] = _dot(c_ref[...], hs_ref[...]) * scale
    ki_ref[...] = _dot(s_ref[...], hd_ref[...]) * scale
    kn_ref[...] = _dot(sg_ref[...], hs_ref[...]) * (1.0 / n)


def _hy_spectrum(length, cos_t, sin_t, sign_rows, hs, hd):
    tf = min(256, length)
    width = hs.shape[1]
    return pl.pallas_call(
        functools.partial(_hy_spec_kernel, length=length, tf=tf),
        out_shape=[jax.ShapeDtypeStruct((length, width), F32), jax.ShapeDtypeStruct((length, width), F32),
                   jax.ShapeDtypeStruct((SUBLANES, width), F32)],
        grid=(length // tf,),
        in_specs=[
            pl.BlockSpec((tf, length), lambda i: (i, 0)),
            pl.BlockSpec((tf, length), lambda i: (i, 0)),
            pl.BlockSpec((SUBLANES, length), lambda i: (0, 0)),
            pl.BlockSpec((length, width), lambda i: (0, 0)),
            pl.BlockSpec((length, width), lambda i: (0, 0)),
        ],
        out_specs=[pl.BlockSpec((tf, width), lambda i: (i, 0)), pl.BlockSpec((tf, width), lambda i: (i, 0)),
                   pl.BlockSpec((SUBLANES, width), lambda i: (0, 0))],
        compiler_params=_cparams("arbitrary"),
        name="hy_spectrum",
    )(cos_t, sin_t, sign_rows, hs, hd)


def _hyena_kernel(ve_ref, x0e_ref, x1e_ref, vo_ref, x0o_ref, x1o_ref, sw_ref, sb_ref, skip_ref, sg_ref, kn_ref,
                  cr_ref, sr_ref, cc_ref, sc_ref, kr_ref, ki_ref, o_ref,
                  ue_ref, uo_ref, x0e_s, x0o_s, acce_ref, acco_ref, nyq_ref, *, half):
    f = pl.program_id(1)
    e = E_HY
    n = min(HY_ROWS, half)
    halo = 2 * SUBLANES
    row = lax.broadcasted_iota(jnp.int32, (n, e), 0)

    @pl.when(f == 0)
    def _():
        def chunk(c, carry):
            r0 = pl.multiple_of(c * n, n)
            rows = pl.ds(r0, n)
            before = pl.ds(pl.multiple_of(jnp.maximum(r0 - halo, 0), halo), halo)
            after = pl.ds(pl.multiple_of(jnp.minimum(r0 + n, half - halo), halo), halo)

            def short_conv(even_ref, odd_ref, j):
                cols = slice(j * e, (j + 1) * e)
                xe = even_ref[rows, :].astype(F32)
                xo = odd_ref[rows, :].astype(F32)
                odd_before = jnp.where(c > 0, odd_ref[before, :].astype(F32)[halo - 1:halo], 0.0)
                even_after = jnp.where(c < half // n - 1, even_ref[after, :].astype(F32)[0:1], 0.0)
                odd_prev = jnp.where(row == 0, odd_before, pltpu.roll(xo, 1, 0))
                even_next = jnp.where(row == n - 1, even_after, pltpu.roll(xe, n - 1, 0))
                w0, w1, w2, b = sw_ref[0:1, cols], sw_ref[1:2, cols], sw_ref[2:3, cols], sb_ref[:, cols]
                return w0 * odd_prev + w1 * xe + w2 * xo + b, w0 * xe + w1 * xo + w2 * even_next + b

            v_e, v_o = short_conv(ve_ref, vo_ref, 0)
            x0_e, x0_o = short_conv(x0e_ref, x0o_ref, 1)
            x1_e, x1_o = short_conv(x1e_ref, x1o_ref, 2)
            u_e = v_e * x1_e
            u_o = v_o * x1_o
            ue_ref[rows, :] = u_e.astype(BF16)
            uo_ref[rows, :] = u_o.astype(BF16)
            x0e_s[rows, :] = x0_e.astype(BF16)
            x0o_s[rows, :] = x0_o.astype(BF16)
            acce_ref[rows, :] = u_e * skip_ref[...]
            acco_ref[rows, :] = u_o * skip_ref[...]
            return carry

        lax.fori_loop(0, half // n, chunk, 0)
        nyq_e = _dot(sg_ref[...], ue_ref[...])
        nyq_o = _dot(sg_ref[...], uo_ref[...])
        kn = kn_ref[...]
        nyq_ref[:, :e] = nyq_e * kn[:, :e] + nyq_o * kn[:, 2 * e:]
        nyq_ref[:, e:] = nyq_e * kn[:, e:2 * e] + nyq_o * kn[:, :e]

    u_e = ue_ref[...]
    u_o = uo_ref[...]
    er, ei = _dot(cr_ref[...], u_e), _dot(sr_ref[...], u_e)
    pr, pi = _dot(cr_ref[...], u_o), _dot(sr_ref[...], u_o)
    kr = kr_ref[...]
    ki = ki_ref[...]

    def spectral(xr, xi, x):
        cols = slice(x * e, (x + 1) * e)
        return xr * kr[:, cols] + xi * ki[:, cols], xi * kr[:, cols] - xr * ki[:, cols]

    a_r, a_i = spectral(er, ei, 0)
    b_r, b_i = spectral(pr, pi, 2)
    acce_ref[...] += _dot(cc_ref[...], (a_r + b_r).astype(BF16)) + _dot(sc_ref[...], (a_i + b_i).astype(BF16))
    a_r, a_i = spectral(er, ei, 1)
    b_r, b_i = spectral(pr, pi, 0)
    acco_ref[...] += _dot(cc_ref[...], (a_r + b_r).astype(BF16)) + _dot(sc_ref[...], (a_i + b_i).astype(BF16))

    @pl.when(f == pl.num_programs(1) - 1)
    def _():
        sign = (1 - 2 * (row & 1)).astype(F32)
        nyq_e = sign * nyq_ref[0:1, :e]
        nyq_o = sign * nyq_ref[0:1, e:]

        def chunk(c, carry):
            rows = pl.ds(pl.multiple_of(c * n, n), n)
            o_ref[rows, :e] = (x0e_s[rows, :].astype(F32) * (acce_ref[rows, :] + nyq_e)).astype(BF16)
            o_ref[rows, e:] = (x0o_s[rows, :].astype(F32) * (acco_ref[rows, :] + nyq_o)).astype(BF16)
            return carry

        lax.fori_loop(0, half // n, chunk, 0)


def _hyena(p, layer, short_w, short_b, skip, tables, spectrum):
    nb, length, width = p.shape
    half = length // 2
    cos_t, sin_t, sign_rows = tables
    kr, ki, kn = spectrum
    tf = min(512, half)
    odd = width // E_HY
    col = lambda j: pl.BlockSpec((None, half, E_HY), lambda b, f: (b, 0, j))
    p2 = p.reshape(nb, half, 2 * width)
    out = pl.pallas_call(
        functools.partial(_hyena_kernel, half=half),
        out_shape=jax.ShapeDtypeStruct((nb, half, 2 * E_HY), BF16),
        grid=(nb, half // tf),
        in_specs=[
            col(COL_HY_V), col(COL_HY_X0), col(COL_HY_X1),
            col(odd + COL_HY_V), col(odd + COL_HY_X0), col(odd + COL_HY_X1),
            pl.BlockSpec((None, 3, 3 * E_HY), lambda b, f: (layer, 0, 0)),
            pl.BlockSpec((None, 1, 3 * E_HY), lambda b, f: (layer, 0, 0)),
            pl.BlockSpec((None, 1, E_HY), lambda b, f: (layer, 0, 0)),
            pl.BlockSpec((SUBLANES, half), lambda b, f: (0, 0)),
            pl.BlockSpec((SUBLANES, 3 * E_HY), lambda b, f: (0, 0)),
            pl.BlockSpec((tf, half), lambda b, f: (f, 0)),
            pl.BlockSpec((tf, half), lambda b, f: (f, 0)),
            pl.BlockSpec((half, tf), lambda b, f: (0, f)),
            pl.BlockSpec((half, tf), lambda b, f: (0, f)),
            pl.BlockSpec((tf, 3 * E_HY), lambda b, f: (f, 0)),
            pl.BlockSpec((tf, 3 * E_HY), lambda b, f: (f, 0)),
        ],
        out_specs=pl.BlockSpec((None, half, 2 * E_HY), lambda b, f: (b, 0, 0)),
        scratch_shapes=[pltpu.VMEM((half, E_HY), BF16)] * 4 + [pltpu.VMEM((half, E_HY), F32)] * 2
                       + [pltpu.VMEM((SUBLANES, 2 * E_HY), F32)],
        compiler_params=_cparams("parallel", "arbitrary"),
        name="hyena",
    )(p2, p2, p2, p2, p2, p2, short_w, short_b, skip, sign_rows, kn, cos_t, sin_t, cos_t, sin_t, kr, ki)
    return out.reshape(nb, length, E_HY)


LRU_PAD = SUBLANES
LRU_CHUNK = 128


def _lru_kernel(xc_ref, gc_ref, xl_ref, gl_ref, cw_ref, cb_ref, wg_ref, bg_ref, lam_ref, yc_ref, yl_ref,
                xpad, a_cum, b_cum, h_sum, *, tc, tl):
    off_c = LRU_PAD
    off_l = off_c + tc + LRU_PAD
    n = LRU_CHUNK
    nj = E_LRU // LANES
    n_chunks_c = tc // n
    n_chunks = (tc + tl) // n
    groups_c = tc // SUBLANES
    groups = (tc + tl) // SUBLANES

    xpad[...] = jnp.zeros_like(xpad)
    xpad[off_c:off_c + tc, :] = xc_ref[...].astype(F32)
    xpad[off_l:off_l + tl, :] = xl_ref[...].astype(F32)

    sub = lax.broadcasted_iota(jnp.int32, (n // SUBLANES, SUBLANES, LANES), 1)

    for d in range(2):
        lam = lam_ref[d:d + 1, :]
        softplus_neg = jnp.maximum(-lam, 0.0) + jnp.log1p(jnp.exp(-jnp.abs(lam)))

        def phase_a(k, carry, d=d, softplus_neg=softplus_neg):
            start = pl.multiple_of(off_c + k * n + jnp.where(k >= n_chunks_c, LRU_PAD, 0), SUBLANES)
            for j in range(nj):
                cols = slice(j * LANES, (j + 1) * LANES)
                win = xpad[pl.ds(start - LRU_PAD, n + 2 * LRU_PAD), cols]
                xc = cb_ref[d:d + 1, cols] + jnp.zeros((n, LANES), F32)
                for tap in range(LRU_CONV):
                    back = LRU_CONV - 1 - tap
                    shift = back if d == 0 else (n + 2 * LRU_PAD - back) % (n + 2 * LRU_PAD)
                    src = win if back == 0 else pltpu.roll(win, shift, 0)
                    xc = xc + cw_ref[d, tap:tap + 1, cols] * src[LRU_PAD:LRU_PAD + n]
                gates = _dot(xc.astype(BF16), wg_ref[d, j]) + bg_ref[d, j:j + 1, :]
                r = jax.nn.sigmoid(gates[:, :LANES])
                i = jax.nn.sigmoid(gates[:, LANES:])
                log_a = (-LRU_C) * r * softplus_neg[:, cols]
                a = jnp.exp(log_a)
                b = jnp.sqrt(-jnp.tanh(log_a) * (a * a + 1.0)) * (i * xc)
                a = a.reshape(n // SUBLANES, SUBLANES, LANES)
                b = b.reshape(n // SUBLANES, SUBLANES, LANES)
                for s in (1, 2, 4):
                    if d == 0:
                        keep = sub >= s
                        a_prev = pltpu.roll(a, s, 1)
                        b_prev = pltpu.roll(b, s, 1)
                    else:
                        keep = sub < SUBLANES - s
                        a_prev = pltpu.roll(a, SUBLANES - s, 1)
                        b_prev = pltpu.roll(b, SUBLANES - s, 1)
                    b = jnp.where(keep, a * b_prev + b, b)
                    a = jnp.where(keep, a * a_prev, a)
                a_cum[pl.ds(start, n), cols] = a.reshape(n, LANES)
                b_cum[pl.ds(start, n), cols] = b.reshape(n, LANES)
            return carry

        lax.fori_loop(0, n_chunks, phase_a, 0)

        def phase_b(g, h_prev, d=d):
            if d == 0:
                row = off_c + g * SUBLANES + jnp.where(g >= groups_c, LRU_PAD, 0)
            else:
                row = jnp.where(g < groups_c, off_c + (groups_c - 1 - g) * SUBLANES,
                                off_l + (groups - 1 - g) * SUBLANES)
            rows = pl.ds(pl.multiple_of(row, SUBLANES), SUBLANES)
            h = a_cum[rows, :] * h_prev + b_cum[rows, :]
            if d == 0:
                h_sum[rows, :] = h
                return jnp.broadcast_to(h[SUBLANES - 1:SUBLANES, :], (SUBLANES, E_LRU))
            h_sum[rows, :] = h_sum[rows, :] + h
            return jnp.broadcast_to(h[0:1, :], (SUBLANES, E_LRU))

        lax.fori_loop(0, groups, phase_b, jnp.zeros((SUBLANES, E_LRU), F32), unroll=4)

    yc_ref[...] = (h_sum[off_c:off_c + tc, :] * _gelu_tanh(gc_ref[...].astype(F32))).astype(BF16)

    def phase_c(k, carry):
        r0 = pl.multiple_of(k * 256, 256)
        yl_ref[pl.ds(r0, 256), :] = (h_sum[pl.ds(off_l + r0, 256), :]
                                     * _gelu_tanh(gl_ref[pl.ds(r0, 256), :].astype(F32))).astype(BF16)
        return carry

    lax.fori_loop(0, tl // 256, phase_c, 0)


def _lru_gate_weights(wr, wi, br, bi):
    per = LANES // LRU_BLOCK_DIM
    nj = E_LRU // LANES

    def lane_blocks(w):
        w = w.reshape(DEPTH, 2, nj, per, LRU_BLOCK_DIM, LRU_BLOCK_DIM)
        eye = jnp.eye(per, dtype=w.dtype)
        full = w[:, :, :, :, :, None, :] * eye[None, None, None, :, None, :, None]
        return full.reshape(DEPTH, 2, nj, LANES, LANES)

    wg = jnp.concatenate([lane_blocks(wr), lane_blocks(wi)], axis=-1).astype(BF16)
    bg = jnp.concatenate([br.reshape(DEPTH, 2, nj, LANES), bi.reshape(DEPTH, 2, nj, LANES)], axis=-1)
    return wg, bg


def _lru(p_ctx, p_lat, layer, conv_w, conv_b, wg, bg, lam):
    nb, tc, _ = p_ctx.shape
    tl = p_lat.shape[1]
    rows = tc + tl + 3 * LRU_PAD
    col = lambda t, j: pl.BlockSpec((None, t, E_LRU), lambda b: (b, 0, j))
    par = lambda shape: pl.BlockSpec((None,) + shape, lambda b: (layer,) + (0,) * len(shape))
    return pl.pallas_call(
        functools.partial(_lru_kernel, tc=tc, tl=tl),
        out_shape=[jax.ShapeDtypeStruct((nb, tc, E_LRU), BF16), jax.ShapeDtypeStruct((nb, tl, E_LRU), BF16)],
        grid=(nb,),
        in_specs=[col(tc, COL_LRU_X), col(tc, COL_LRU_G), col(tl, COL_LRU_X), col(tl, COL_LRU_G),
                  par((2, LRU_CONV, E_LRU)), par((2, E_LRU)), par((2, E_LRU // LANES, LANES, 2 * LANES)),
                  par((2, E_LRU // LANES, 2 * LANES)), par((2, E_LRU))],
        out_specs=[pl.BlockSpec((None, tc, E_LRU), lambda b: (b, 0, 0)),
                   pl.BlockSpec((None, tl, E_LRU), lambda b: (b, 0, 0))],
        scratch_shapes=[pltpu.VMEM((rows, E_LRU), F32)] * 4,
        compiler_params=_cparams("parallel"),
        name="lru",
    )(p_ctx, p_ctx, p_lat, p_lat, conv_w, conv_b, wg, bg, lam)


def _cumsum_chunk(x, sub, reverse):
    groups = x.shape[0] // SUBLANES
    x3 = x.reshape(groups, SUBLANES, x.shape[1])
    for s in (1, 2, 4):
        if reverse:
            x3 = x3 + jnp.where(sub < SUBLANES - s, pltpu.roll(x3, SUBLANES - s, 1), 0.0)
        else:
            x3 = x3 + jnp.where(sub >= s, pltpu.roll(x3, s, 1), 0.0)
    edge = 0 if reverse else SUBLANES - 1
    blocks = [None] * groups
    total = None
    for g in (range(groups - 1, -1, -1) if reverse else range(groups)):
        blk = x3[g] if total is None else x3[g] + total
        blocks[g] = blk
        total = blk[edge:edge + 1, :]
    return blocks


def _hgrn_kernel(qc_ref, ffc_ref, fbc_ref, ic_ref, ogc_ref, ql_ref, ffl_ref, fbl_ref, il_ref, ogl_ref,
                 lb_ref, ng_ref, yc_ref, yl_ref, o_acc, o_f, o_b, qg_s, kd_s, v_s, dec_s, *, layer, tc, tl):
    ch = HG_CHUNK
    raw = [lb_ref[i:i + 1, :] for i in range(DEPTH)]
    top = functools.reduce(jnp.maximum, raw)
    ex = [jnp.exp(r - top) for r in raw]
    tot = functools.reduce(lambda a, b: a + b, ex)
    lb = jnp.zeros((1, E_HG), F32)
    for i in range(1, layer + 1):
        lb = lb + ex[i] / tot

    rt = lax.broadcasted_iota(jnp.int32, (ch, ch), 0)
    ct = lax.broadcasted_iota(jnp.int32, (ch, ch), 1)
    tri = [(rt >= ct), (rt <= ct)]
    sub = lax.broadcasted_iota(jnp.int32, (ch // SUBLANES, SUBLANES, HG_DK), 1)
    n_c, n_l = tc // ch, tl // ch
    n_all = n_c + n_l
    mid_g = (ch // 2) // SUBLANES
    last = SUBLANES - 1

    def intra(q_ref, ff_ref, fb_ref, i_ref, n_chunks, chunk0):
        def body(it, carry):
            chains = []
            for c in (it * HG_TRIP + i for i in range(HG_TRIP)):
                rows = pl.ds(pl.multiple_of(c * ch, ch), ch)
                srows = pl.ds(pl.multiple_of((chunk0 + c) * ch, ch), ch)
                drows = pl.ds(pl.multiple_of((chunk0 + c) * SUBLANES, SUBLANES), SUBLANES)
                for hd in range(HG_HEADS):
                    cols = slice(hd * HG_DK, (hd + 1) * HG_DK)
                    lbh = lb[:, cols]
                    q = _silu(q_ref[rows, cols].astype(F32))
                    v = i_ref[rows, cols]
                    v_s[srows, cols] = v
                    for d, f_ref in enumerate((ff_ref, fb_ref)):
                        f = lbh + (1.0 - lbh) * jax.nn.sigmoid(f_ref[rows, cols].astype(F32))
                        k = 1.0 - f
                        blocks = _cumsum_chunk(jnp.log(f), sub, reverse=d == 1)
                        b = jnp.concatenate(blocks, axis=0)
                        if d == 0:
                            b_last, b_mid = blocks[-1][last:last + 1], blocks[mid_g][0:1]
                        else:
                            b_last, b_mid = blocks[0][0:1], blocks[mid_g - 1][last:last + 1]
                        e_mid = jnp.exp(b - b_mid)
                        qs = (q * e_mid).astype(BF16)
                        ks = (k * jnp.exp(b_mid - b)).astype(BF16)
                        qg_s[d, srows, cols] = (q * e_mid * jnp.exp(b_mid)).astype(BF16)
                        kd_s[d, srows, cols] = (k * jnp.exp(b_last - b)).astype(BF16)
                        dec_s[d, drows, cols] = jnp.broadcast_to(jnp.exp(b_last), (SUBLANES, HG_DK))
                        chains.append((srows, cols, d, qs, ks, v))
            scores = [_dot_nt(qs, ks) for _, _, _, qs, ks, _ in chains]
            masked = [jnp.where(tri[chain[2]], s, 0.0).astype(BF16) for chain, s in zip(chains, scores)]
            outs = [_dot(s, chain[5]) for chain, s in zip(chains, masked)]
            for i in range(0, len(chains), 2):
                srows, cols = chains[i][:2]
                o_acc[srows, cols] = outs[i] + outs[i + 1]
            return carry

        lax.fori_loop(0, n_chunks // HG_TRIP, body, 0)

    intra(qc_ref, ffc_ref, fbc_ref, ic_ref, n_c, 0)
    intra(ql_ref, ffl_ref, fbl_ref, il_ref, n_l, n_c)

    def inter(it, states):
        new = []
        for d in range(2):
            g = it if d == 0 else jnp.where(it < n_c, n_c - 1 - it, n_all + n_c - 1 - it)
            rows = pl.ds(pl.multiple_of(g * ch, ch), ch)
            drows = pl.ds(pl.multiple_of(g * SUBLANES, SUBLANES), SUBLANES)
            o_d = o_f if d == 0 else o_b
            for hd in range(HG_HEADS):
                cols = slice(hd * HG_DK, (hd + 1) * HG_DK)
                s_t = states[d * HG_HEADS + hd]
                o_d[rows, cols] = _dot_nt(qg_s[d, rows, cols], s_t.astype(BF16))
                dec = dec_s[d, drows, cols][0:1]
                new.append(s_t * dec + _dot_tn(v_s[rows, cols], kd_s[d, rows, cols]))
        return tuple(new)

    lax.fori_loop(0, n_all, inter, tuple(jnp.zeros((HG_DV, HG_DK), F32) for _ in range(2 * HG_HEADS)), unroll=4)

    def finish(og_ref, y_ref, base, t):
        blk = min(t, 256)

        def body(k, carry):
            r0 = pl.multiple_of(k * blk, blk)
            for hd in range(HG_HEADS):
                cols = slice(hd * HG_DV, (hd + 1) * HG_DV)
                orow = pl.ds(base + r0, blk)
                o = o_acc[orow, cols] + o_f[orow, cols] + o_b[orow, cols]
                y = _rms(o) * ng_ref[...] * _silu(og_ref[pl.ds(r0, blk), cols].astype(F32))
                y_ref[pl.ds(r0, blk), cols] = y.astype(BF16)
            return carry

        lax.fori_loop(0, t // blk, body, 0)

    finish(ogc_ref, yc_ref, 0, tc)
    finish(ogl_ref, yl_ref, tc, tl)


def _hgrn(p_ctx, p_lat, layer, lower_bounds, norm_g):
    nb, tc, _ = p_ctx.shape
    tl = p_lat.shape[1]
    t = tc + tl
    col = lambda n, j: pl.BlockSpec((None, n, E_HG), lambda b: (b, 0, j), pipeline_mode=pl.Buffered(1))
    names = (COL_HG_Q, COL_HG_FF, COL_HG_FB, COL_HG_I, COL_HG_OG)
    return pl.pallas_call(
        functools.partial(_hgrn_kernel, layer=layer, tc=tc, tl=tl),
        out_shape=[jax.ShapeDtypeStruct((nb, tc, E_HG), BF16), jax.ShapeDtypeStruct((nb, tl, E_HG), BF16)],
        grid=(nb,),
        in_specs=[col(tc, j) for j in names] + [col(tl, j) for j in names] + [
            pl.BlockSpec((DEPTH, E_HG), lambda b: (0, 0)),
            pl.BlockSpec((None, 1, HG_DV), lambda b: (layer, 0, 0)),
        ],
        out_specs=[pl.BlockSpec((None, tc, E_HG), lambda b: (b, 0, 0)),
                   pl.BlockSpec((None, tl, E_HG), lambda b: (b, 0, 0))],
        scratch_shapes=[pltpu.VMEM((t, E_HG), F32), pltpu.VMEM((t, E_HG), F32), pltpu.VMEM((t, E_HG), F32),
                        pltpu.VMEM((2, t, E_HG), BF16), pltpu.VMEM((2, t, E_HG), BF16), pltpu.VMEM((t, E_HG), BF16),
                        pltpu.VMEM((2, t // HG_CHUNK * SUBLANES, E_HG), F32)],
        compiler_params=_cparams("parallel"),
        name="hgrn",
    )(*([p_ctx] * 5 + [p_lat] * 5 + [lower_bounds, norm_g]))


def _merge_kernel(gh_ref, gl_ref, gg_ref, yh_ref, yl_ref, yg_ref, wh_ref, wl_ref, wg_ref, wo_ref,
                  h_ref, g1_ref, gain_ref, o_ref, *t_ref, n_cols, n_rows):
    def branch(g_ref, y_ref, w_ref):
        return jax.nn.sigmoid(g_ref[...].astype(F32)) * _dot(y_ref[...], w_ref[...])

    m = branch(gh_ref, yh_ref, wh_ref) + branch(gl_ref, yl_ref, wl_ref) + branch(gg_ref, yg_ref, wg_ref)
    r = _rms(_dot(m.astype(BF16), wo_ref[...])) * (gain_ref[...] * g1_ref[...])
    if n_cols:
        for j in range(D_MODEL // LANES):
            lanes = slice(j * LANES, (j + 1) * LANES)
            for c in range(n_cols):
                t_ref[0][j, pl.ds(c, n_rows, stride=n_cols), :] = r[c * n_rows:(c + 1) * n_rows, lanes]
            o_ref[:, :, lanes] = h_ref[:, :, lanes] + t_ref[0][j].reshape(n_rows, n_cols, LANES)
    else:
        o_ref[...] = h_ref[...] + r


def _merge(p, y_hy, y_lru, y_hg, w_hy, w_lru, w_hg, w_out, h, ada4, gains4, layer, *, ctx_row, col_major):
    nb, t, _ = h.shape
    tm = 512
    row_fn = (lambda b: b) if ctx_row is None else (lambda b: ctx_row)
    if col_major:
        n_rows = t // GRID_W
        n_cols = tm // n_rows
        h = h.reshape(nb, n_rows, GRID_W, D_MODEL)
        h_spec = pl.BlockSpec((None, n_rows, n_cols, D_MODEL), lambda b, i: (b, 0, i, 0))
        scratch = [pltpu.VMEM((D_MODEL // LANES, tm, LANES), F32)]
    else:
        n_rows = n_cols = 0
        h_spec = pl.BlockSpec((None, tm, D_MODEL), lambda b, i: (b, i, 0))
        scratch = []
    gate = lambda k: pl.BlockSpec((None, tm, D_MODEL), lambda b, i: (b, i, COL_GATES + k))
    y_spec = pl.BlockSpec((None, tm, E_HY), lambda b, i: (b, i, 0))
    w_spec = pl.BlockSpec((None, E_HY, D_MODEL), lambda b, i: (layer, 0, 0))
    out = pl.pallas_call(
        functools.partial(_merge_kernel, n_cols=n_cols, n_rows=n_rows),
        out_shape=jax.ShapeDtypeStruct(h.shape, F32),
        grid=(nb, t // tm),
        in_specs=[gate(0), gate(1), gate(2), y_spec, y_spec, y_spec, w_spec, w_spec, w_spec,
                  pl.BlockSpec((None, D_MODEL, D_MODEL), lambda b, i: (layer, 0, 0)),
                  h_spec, _ada_spec(layer, row_fn, 2, 2), _gain_spec(layer, 1, 2)],
        out_specs=h_spec,
        scratch_shapes=scratch,
        compiler_params=_cparams("parallel", "parallel"),
        name="merge",
    )(p, p, p, y_hy, y_lru, y_hg, w_hy, w_lru, w_hg, w_out, h, ada4, gains4)
    return out.reshape(nb, t, D_MODEL)


def _mlp_kernel(h_ref, sh_ref, sc_ref, g2_ref, gain2_ref, gain3_ref, w1_ref, w2_ref, o_ref):
    h = h_ref[...]
    u = (_rms(h) * (gain2_ref[...] * (1.0 + sc_ref[...])) + sh_ref[...]).astype(BF16)
    acc = None
    for k in range(0, D_FF, MLP_FF_BLOCK):
        a = jnp.maximum(_dot(u, w1_ref[:, k:k + MLP_FF_BLOCK]), 0.0)
        part = _dot((a * a).astype(BF16), w2_ref[k:k + MLP_FF_BLOCK, :])
        acc = part if acc is None else acc + part
    o_ref[...] = h + (g2_ref[...] * gain3_ref[...]) * _rms(acc)


def _mlp(h, ada4, gains4, w1_b, w2_b, layer, *, ctx_row):
    nb, t, _ = h.shape
    tm = 512
    row_fn = (lambda b: b) if ctx_row is None else (lambda b: ctx_row)
    h_spec = pl.BlockSpec((None, tm, D_MODEL), lambda b, i: (b, i, 0))
    return pl.pallas_call(
        _mlp_kernel,
        out_shape=jax.ShapeDtypeStruct(h.shape, F32),
        grid=(nb, t // tm),
        in_specs=[h_spec, _ada_spec(layer, row_fn, 3, 2), _ada_spec(layer, row_fn, 4, 2),
                  _ada_spec(layer, row_fn, 5, 2), _gain_spec(layer, 2, 2), _gain_spec(layer, 3, 2),
                  pl.BlockSpec((None, D_MODEL, D_FF), lambda b, i: (layer, 0, 0), pipeline_mode=pl.Buffered(1)),
                  pl.BlockSpec((None, D_FF, D_MODEL), lambda b, i: (layer, 0, 0), pipeline_mode=pl.Buffered(1))],
        out_specs=h_spec,
        compiler_params=_cparams("parallel", "parallel"),
        name="mlp",
    )(h, ada4, ada4, ada4, gains4, gains4, w1_b, w2_b)


def kernel(x, c, ctx, c_ctx, w_ada, b_ada, norm_gains, w_in, hy_short_w, hy_short_b, hy_ff_w1, hy_ff_b1,
           hy_ff_w2, hy_ff_b2, hy_ff_w3, hy_freq, hy_skip, lru_conv_w, lru_conv_b, lru_wr, lru_br, lru_wi,
           lru_bi, lru_lambda, hg_lower_bounds, hg_norm_g, w_proj_hy, w_proj_lru, w_proj_hg, w_out,
           w_mlp1, w_mlp2):
    nb, seq, _ = x.shape
    ctx_len = ctx.shape[1]
    assert seq % GRID_W == 0

    w_in_b, w_out_b = w_in.astype(BF16), w_out.astype(BF16)
    w_hy_b, w_lru_b, w_hg_b = w_proj_hy.astype(BF16), w_proj_lru.astype(BF16), w_proj_hg.astype(BF16)
    w1_b, w2_b = w_mlp1.astype(BF16), w_mlp2.astype(BF16)

    ctx_row = nb
    ada_rows = -(-(nb + 1) // SUBLANES) * SUBLANES
    cin = jnp.concatenate([c, c_ctx[None], jnp.zeros((ada_rows - nb - 1, D_MODEL), F32)], axis=0)
    ada4 = _ada_call(cin, w_ada, b_ada).reshape(DEPTH, ada_rows, 1, N_ADA * D_MODEL)
    gains4 = norm_gains.reshape(DEPTH, 4, 1, D_MODEL)

    hy_w1p = jnp.pad(hy_ff_w1, ((0, 0), (0, LANES - hy_ff_w1.shape[1]), (0, 0)))
    hy_b1 = hy_ff_b1.reshape(DEPTH, 1, HY_ORDER)
    hy_b2 = hy_ff_b2.reshape(DEPTH, 1, HY_ORDER)
    hy_sb = hy_short_b.reshape(DEPTH, 1, 3 * E_HY)
    hy_sk = hy_skip.reshape(DEPTH, 1, E_HY)
    lru_wg, lru_bg = _lru_gate_weights(lru_wr, lru_wi, lru_br, lru_bi)
    hg_ng = hg_norm_g.reshape(DEPTH, 1, HG_DV)

    tables = {seq // 2: _dft_tables(seq // 2), ctx_len // 2: _dft_tables(ctx_len // 2)}

    def hyena(p, layer):
        length = p.shape[1]
        hs, hd = _hy_taps(length, layer, hy_w1p, hy_b1, hy_ff_w2, hy_b2, hy_ff_w3, hy_freq)
        spectrum = _hy_spectrum(length // 2, *tables[length // 2], hs, hd)
        return _hyena(p, layer, hy_short_w, hy_sb, hy_sk, tables[length // 2], spectrum)

    h_lat = x
    h_ctx = ctx.reshape(1, nb * ctx_len, D_MODEL)
    for layer in range(DEPTH):
        need_ctx = layer < DEPTH - 1
        col_major = layer % 2 == 1
        p_lat = _in_proj(h_lat, ada4, gains4, w_in_b, layer, ctx_row=None, col_major=col_major)
        p_ctx_flat = _in_proj(h_ctx, ada4, gains4, w_in_b, layer, ctx_row=ctx_row, col_major=False)
        p_ctx = p_ctx_flat.reshape(nb, ctx_len, IN_WIDTH)

        y_hy_lat = hyena(p_lat, layer)
        y_lru_ctx, y_lru_lat = _lru(p_ctx, p_lat, layer, lru_conv_w, lru_conv_b, lru_wg, lru_bg, lru_lambda)
        y_hg_ctx, y_hg_lat = _hgrn(p_ctx, p_lat, layer, hg_lower_bounds, hg_ng)
        proj = (w_hy_b, w_lru_b, w_hg_b, w_out_b)
        h_lat = _merge(p_lat, y_hy_lat, y_lru_lat, y_hg_lat, *proj, h_lat, ada4, gains4, layer,
                       ctx_row=None, col_major=col_major)
        if need_ctx:
            flat = lambda y: y.reshape(1, nb * ctx_len, y.shape[-1])
            y_hy_ctx = hyena(p_ctx, layer)
            h_ctx = _merge(p_ctx_flat, flat(y_hy_ctx), flat(y_lru_ctx), flat(y_hg_ctx), *proj, h_ctx, ada4,
                           gains4, layer, ctx_row=ctx_row, col_major=False)

        h_lat = _mlp(h_lat, ada4, gains4, w1_b, w2_b, layer, ctx_row=None)
        if need_ctx:
            h_ctx = _mlp(h_ctx, ada4, gains4, w1_b, w2_b, layer, ctx_row=ctx_row)
    return h_lat
```

```python
import functools
import math

import jax
import jax.numpy as jnp
from jax import lax
from jax.experimental import pallas as pl
from jax.experimental.pallas import tpu as pltpu

F32 = jnp.float32
BF16 = jnp.bfloat16
HIGHEST = lax.Precision.HIGHEST

D_MODEL = 1024
DEPTH = 2
GRID_W = 64
EPS = 1e-6

E_HY = 512
HY_EMB_BANDS = 8
HY_ORDER = 64
HY_DECAY_TARGET = 1e-2
HY_FAST_DECAY = 0.3
HY_SLOW_DECAY = 1.5
HY_ROWS = 64
DFT_COARSE = 64

E_LRU = 512
LRU_BLOCKS = 8
LRU_BLOCK_DIM = E_LRU // LRU_BLOCKS
LRU_CONV = 4
LRU_C = 8.0

HG_HEADS = 4
HG_DK = 128
HG_DV = 128
E_HG = HG_HEADS * HG_DV
HG_CHUNK = 64
HG_TRIP = 4

D_FF = 4 * D_MODEL
IN_WIDTH = 8 * D_MODEL
N_ADA = 6

COL_HY_V, COL_HY_X0, COL_HY_X1 = 0, 1, 2
COL_LRU_X, COL_LRU_G = 3, 4
COL_HG_Q, COL_HG_FF, COL_HG_FB, COL_HG_I, COL_HG_OG = 5, 6, 7, 8, 9
COL_GATES = 5

LANES = 128
SUBLANES = 8
VMEM_LIMIT = 56 * 1024 * 1024
NORM_ROWS = 512
MLP_FF_BLOCK = 1024


def _cparams(*sem):
    return pltpu.CompilerParams(dimension_semantics=sem, vmem_limit_bytes=VMEM_LIMIT)


def _rms(x):
    return x * lax.rsqrt(jnp.mean(x * x, axis=-1, keepdims=True) + EPS)


def _silu(x):
    return x * jax.nn.sigmoid(x)


def _gelu_tanh(x):
    return x * (0.5 * (1.0 + jnp.tanh(math.sqrt(2.0 / math.pi) * (x + 0.044715 * (x * x * x)))))


def _dot(a, b):
    return jnp.dot(a, b, preferred_element_type=F32)


def _dot_nt(a, b):
    return lax.dot_general(a, b, (((1,), (1,)), ((), ())), preferred_element_type=F32)


def _dot_tn(a, b):
    return lax.dot_general(a, b, (((0,), (0,)), ((), ())), preferred_element_type=F32)


def _ada_spec(layer, row_fn, chunk, nargs):
    if nargs == 2:
        return pl.BlockSpec((None, None, 1, D_MODEL), lambda b, i: (layer, row_fn(b), 0, chunk))
    return pl.BlockSpec((None, None, 1, D_MODEL), lambda b, i, j: (layer, row_fn(b), 0, chunk))


def _gain_spec(layer, idx, nargs):
    if nargs == 2:
        return pl.BlockSpec((None, None, 1, D_MODEL), lambda b, i: (layer, idx, 0, 0))
    return pl.BlockSpec((None, None, 1, D_MODEL), lambda b, i, j: (layer, idx, 0, 0))


def _ada_kernel(c_ref, w_ref, b_ref, o_ref):
    s = _silu(c_ref[...])
    o_ref[...] = _dot(s.astype(BF16), w_ref[...].astype(BF16)) + b_ref[...]


def _ada_call(cin, w_ada_b, b_ada):
    tn = D_MODEL
    rows = cin.shape[0]
    return pl.pallas_call(
        _ada_kernel,
        out_shape=jax.ShapeDtypeStruct((DEPTH, rows, N_ADA * D_MODEL), F32),
        grid=(DEPTH, N_ADA * D_MODEL // tn),
        in_specs=[
            pl.BlockSpec((rows, D_MODEL), lambda l, j: (0, 0)),
            pl.BlockSpec((None, D_MODEL, tn), lambda l, j: (l, 0, j)),
            pl.BlockSpec((None, 1, tn), lambda l, j: (l, 0, j)),
        ],
        out_specs=pl.BlockSpec((None, rows, tn), lambda l, j: (l, 0, j)),
        compiler_params=_cparams("parallel", "parallel"),
        name="ada",
    )(cin, w_ada_b, b_ada.reshape(DEPTH, 1, N_ADA * D_MODEL))


def _inproj_kernel(h_ref, sh_ref, sc_ref, g_ref, w_ref, o_ref, u_ref, *t_ref, n_cols, n_rows):
    @pl.when(pl.program_id(2) == 0)
    def _():
        mul = g_ref[...] * (1.0 + sc_ref[...])
        sh = sh_ref[...]
        if n_cols:
            def grid_row(r, carry):
                y = _rms(h_ref[r]) * mul + sh
                rows = pl.ds(pl.multiple_of(r * n_cols, n_cols), n_cols)
                for j in range(D_MODEL // LANES):
                    t_ref[0][j, rows, :] = y[:, j * LANES:(j + 1) * LANES]
                return carry

            lax.fori_loop(0, n_rows, grid_row, 0, unroll=4)
            for j in range(D_MODEL // LANES):
                lanes = slice(j * LANES, (j + 1) * LANES)
                for c in range(n_cols):
                    u_ref[c * n_rows:(c + 1) * n_rows, lanes] = (
                        t_ref[0][j, pl.ds(c, n_rows, stride=n_cols), :].astype(BF16))
        else:
            for k in range(0, u_ref.shape[0], NORM_ROWS):
                u_ref[k:k + NORM_ROWS, :] = (_rms(h_ref[k:k + NORM_ROWS, :]) * mul + sh).astype(BF16)

    o_ref[...] = _dot(u_ref[...], w_ref[...]).astype(BF16)


def _in_proj(h, ada4, gains4, w_in_b, layer, *, ctx_row, col_major):
    nb, t, _ = h.shape
    tm, tn = min(2048, t), 1024
    row_fn = (lambda b: b) if ctx_row is None else (lambda b: ctx_row)
    if col_major:
        n_rows = t // GRID_W
        n_cols = tm // n_rows
        h = h.reshape(nb, n_rows, GRID_W, D_MODEL)
        h_spec = pl.BlockSpec((None, n_rows, n_cols, D_MODEL), lambda b, i, j: (b, 0, i, 0))
        scratch = [pltpu.VMEM((tm, D_MODEL), BF16), pltpu.VMEM((D_MODEL // LANES, tm, LANES), F32)]
    else:
        n_rows = n_cols = 0
        h_spec = pl.BlockSpec((None, tm, D_MODEL), lambda b, i, j: (b, i, 0))
        scratch = [pltpu.VMEM((tm, D_MODEL), BF16)]
    return pl.pallas_call(
        functools.partial(_inproj_kernel, n_cols=n_cols, n_rows=n_rows),
        out_shape=jax.ShapeDtypeStruct((nb, t, IN_WIDTH), BF16),
        grid=(nb, t // tm, IN_WIDTH // tn),
        in_specs=[
            h_spec,
            _ada_spec(layer, row_fn, 0, 3),
            _ada_spec(layer, row_fn, 1, 3),
            _gain_spec(layer, 0, 3),
            pl.BlockSpec((None, D_MODEL, tn), lambda b, i, j: (layer, 0, j)),
        ],
        out_specs=pl.BlockSpec((None, tm, tn), lambda b, i, j: (b, i, j)),
        scratch_shapes=scratch,
        compiler_params=_cparams("parallel", "parallel", "arbitrary"),
        name="in_proj",
    )(h, ada4, ada4, gains4, w_in_b)


def _dft_tables(length):
    n = 2 * length
    idx = jnp.arange(length, dtype=jnp.int32)

    def trig(freqs):
        ang = ((freqs[:, None] * idx[None, :]) % n).astype(F32) * (2.0 * math.pi / n)
        return jnp.cos(ang), jnp.sin(ang)

    c_hi, s_hi = trig(jnp.arange(length // DFT_COARSE, dtype=jnp.int32) * DFT_COARSE)
    c_lo, s_lo = trig(jnp.arange(DFT_COARSE, dtype=jnp.int32))
    cos_t = (c_hi[:, None] * c_lo[None] - s_hi[:, None] * s_lo[None]).reshape(length, length)
    sin_t = (s_hi[:, None] * c_lo[None] + c_hi[:, None] * s_lo[None]).reshape(length, length)
    sign = (1 - 2 * (idx % 2)).astype(F32)
    sign_rows = jnp.zeros((SUBLANES, length), F32).at[0].set(sign)
    return cos_t.astype(BF16), sin_t.astype(BF16), sign_rows.astype(BF16)


def _hy_taps_kernel(w1_ref, b1_ref, w2_ref, b2_ref, w3_ref, fr_ref, hs_ref, hd_ref, *, length, tl):
    row0 = pl.program_id(0) * tl
    m = lax.broadcasted_iota(jnp.int32, (tl, LANES), 0) + row0
    m_wide = lax.broadcasted_iota(jnp.int32, (tl, E_HY), 0) + row0
    lane = lax.broadcasted_iota(jnp.int32, (tl, LANES), 1)
    band_step = (HY_EMB_BANDS - 1 - 1e-4) / (HY_EMB_BANDS - 1)
    band = 1e-4 + jnp.where(lane <= HY_EMB_BANDS, lane - 1, lane - 1 - HY_EMB_BANDS).astype(F32) * band_step
    ch = lax.broadcasted_iota(jnp.int32, (1, E_HY), 1).astype(F32)
    lo = math.log(HY_DECAY_TARGET) / HY_SLOW_DECAY
    hi = math.log(HY_DECAY_TARGET) / HY_FAST_DECAY
    delta = jnp.abs(lo + ch * ((hi - lo) / (E_HY - 1)))

    def taps(offset):
        pos = (2 * m + offset).astype(F32)
        t = pos / max(length - 1, 1)
        ang = band * (2.0 * math.pi * pos / length)
        z = jnp.where(lane == 0, t,
                      jnp.where(lane <= HY_EMB_BANDS, jnp.cos(ang),
                                jnp.where(lane <= 2 * HY_EMB_BANDS, -jnp.sin(ang), 0.0)))
        h = jnp.sin(fr_ref[0:1, :] * (jnp.dot(z, w1_ref[...], precision=HIGHEST, preferred_element_type=F32)
                                      + b1_ref[...]))
        h = jnp.sin(fr_ref[1:2, :] * (jnp.dot(h, w2_ref[...], precision=HIGHEST, preferred_element_type=F32)
                                      + b2_ref[...]))
        h = jnp.dot(h, w3_ref[...], precision=HIGHEST, preferred_element_type=F32)
        decay = jnp.exp(-((2 * m_wide + offset).astype(F32) / max(length - 1, 1)) * delta)
        return h[:, :E_HY] * decay, h[:, E_HY:] * decay

    f_even, b_even = taps(0)
    f_odd, b_odd = taps(1)
    f_prev, b_prev = taps(-1)
    first = m_wide == 0
    forward = (f_even, f_odd, jnp.where(first, b_odd, f_prev))
    backward = (b_even, b_prev, b_odd)
    for x in range(3):
        cols = slice(x * E_HY, (x + 1) * E_HY)
        bwd = jnp.where(first, 0.0, backward[x])
        hs_ref[:, cols] = (forward[x] + bwd).astype(BF16)
        hd_ref[:, cols] = (bwd - forward[x]).astype(BF16)


def _hy_taps(length, layer, w1p, b1, w2, b2, w3, freq):
    half = length // 2
    tl = min(256, half)
    full = lambda shape: pl.BlockSpec((None,) + shape, lambda i: (layer,) + (0,) * len(shape))
    return pl.pallas_call(
        functools.partial(_hy_taps_kernel, length=length, tl=tl),
        out_shape=[jax.ShapeDtypeStruct((half, 3 * E_HY), BF16)] * 2,
        grid=(half // tl,),
        in_specs=[full((LANES, HY_ORDER)), full((1, HY_ORDER)), full((HY_ORDER, HY_ORDER)), full((1, HY_ORDER)),
                  full((HY_ORDER, 2 * E_HY)), full((2, HY_ORDER))],
        out_specs=[pl.BlockSpec((tl, 3 * E_HY), lambda i: (i, 0))] * 2,
        compiler_params=_cparams("parallel"),
        name="hy_taps",
    )(w1p, b1, w2, b2, w3, freq)


def _hy_spec_kernel(c_ref, s_ref, sg_ref, hs_ref, hd_ref, kr_ref, ki_ref, kn_ref, *, length, tf):
    n = 2 * length
    row = lax.broadcasted_iota(jnp.int32, kr_ref.shape, 0) + pl.program_id(0) * tf
    scale = jnp.where(row == 0, 1.0 / n, 2.0 / n)
    kr_ref[...] = _dot(c_ref[...], hs_ref[...]) * scale
    ki_ref[...] = _dot(s_ref[...], hd_ref[...]) * scale
    kn_ref[...] = _dot(sg_ref[...], hs_ref[...]) * (1.0 / n)


def _hy_spectrum(length, cos_t, sin_t, sign_rows, hs, hd):
    tf = min(256, length)
    width = hs.shape[1]
    return pl.pallas_call(
        functools.partial(_hy_spec_kernel, length=length, tf=tf),
        out_shape=[jax.ShapeDtypeStruct((length, width), F32), jax.ShapeDtypeStruct((length, width), F32),
                   jax.ShapeDtypeStruct((SUBLANES, width), F32)],
        grid=(length // tf,),
        in_specs=[
            pl.BlockSpec((tf, length), lambda i: (i, 0)),
            pl.BlockSpec((tf, length), lambda i: (i, 0)),
            pl.BlockSpec((SUBLANES, length), lambda i: (0, 0)),
            pl.BlockSpec((length, width), lambda i: (0, 0)),
            pl.BlockSpec((length, width), lambda i: (0, 0)),
        ],
        out_specs=[pl.BlockSpec((tf, width), lambda i: (i, 0)), pl.BlockSpec((tf, width), lambda i: (i, 0)),
                   pl.BlockSpec((SUBLANES, width), lambda i: (0, 0))],
        compiler_params=_cparams("arbitrary"),
        name="hy_spectrum",
    )(cos_t, sin_t, sign_rows, hs, hd)


def _hyena_kernel(v_ref, x0_ref, x1_ref, sw_ref, sb_ref, skip_ref, sg_ref, kn_ref,
                  cr_ref, sr_ref, cc_ref, sc_ref, kr_ref, ki_ref, o_ref,
                  ue_ref, uo_ref, x0e_s, x0o_s, acce_ref, acco_ref, nyq_ref, *, half):
    f = pl.program_id(1)
    e = E_HY
    n = min(HY_ROWS, half)
    halo = 2 * SUBLANES
    row = lax.broadcasted_iota(jnp.int32, (n, e), 0)
    pick_r = lax.broadcasted_iota(jnp.int32, (2 * n, 2 * n), 0)
    pick_c = lax.broadcasted_iota(jnp.int32, (2 * n, 2 * n), 1)
    split = jnp.where(pick_c == jnp.where(pick_r < n, 2 * pick_r, 2 * (pick_r - n) + 1), 1.0, 0.0).astype(BF16)
    merge = jnp.where(pick_r == jnp.where(pick_c < n, 2 * pick_c, 2 * (pick_c - n) + 1), 1.0, 0.0).astype(BF16)

    @pl.when(f == 0)
    def _():
        def chunk(c, carry):
            r0 = pl.multiple_of(c * n, n)
            rows = pl.ds(r0, n)
            tokens = pl.ds(pl.multiple_of(2 * r0, 2 * n), 2 * n)
            before = pl.ds(pl.multiple_of(jnp.maximum(2 * r0 - halo, 0), halo), halo)
            after = pl.ds(pl.multiple_of(jnp.minimum(2 * r0 + 2 * n, 2 * half - halo), halo), halo)

            def short_conv(ref, j):
                cols = slice(j * e, (j + 1) * e)
                phases = _dot(split, ref[tokens, :])
                xe, xo = phases[:n], phases[n:]
                odd_before = jnp.where(c > 0, ref[before, :].astype(F32)[halo - 1:halo], 0.0)
                even_after = jnp.where(c < half // n - 1, ref[after, :].astype(F32)[0:1], 0.0)
                odd_prev = jnp.where(row == 0, odd_before, pltpu.roll(xo, 1, 0))
                even_next = jnp.where(row == n - 1, even_after, pltpu.roll(xe, n - 1, 0))
                w0, w1, w2, b = sw_ref[0:1, cols], sw_ref[1:2, cols], sw_ref[2:3, cols], sb_ref[:, cols]
                return w0 * odd_prev + w1 * xe + w2 * xo + b, w0 * xe + w1 * xo + w2 * even_next + b

            v_e, v_o = short_conv(v_ref, 0)
            x0_e, x0_o = short_conv(x0_ref, 1)
            x1_e, x1_o = short_conv(x1_ref, 2)
            u_e = v_e * x1_e
            u_o = v_o * x1_o
            ue_ref[rows, :] = u_e.astype(BF16)
            uo_ref[rows, :] = u_o.astype(BF16)
            x0e_s[rows, :] = x0_e.astype(BF16)
            x0o_s[rows, :] = x0_o.astype(BF16)
            acce_ref[rows, :] = u_e * skip_ref[...]
            acco_ref[rows, :] = u_o * skip_ref[...]
            return carry

        lax.fori_loop(0, half // n, chunk, 0, unroll=2)
        nyq_e = _dot(sg_ref[...], ue_ref[...])
        nyq_o = _dot(sg_ref[...], uo_ref[...])
        kn = kn_ref[...]
        nyq_ref[:, :e] = nyq_e * kn[:, :e] + nyq_o * kn[:, 2 * e:]
        nyq_ref[:, e:] = nyq_e * kn[:, e:2 * e] + nyq_o * kn[:, :e]

    u_e = ue_ref[...]
    u_o = uo_ref[...]
    er, ei = _dot(cr_ref[...], u_e), _dot(sr_ref[...], u_e)
    pr, pi = _dot(cr_ref[...], u_o), _dot(sr_ref[...], u_o)
    kr = kr_ref[...]
    ki = ki_ref[...]

    def spectral(xr, xi, x):
        cols = slice(x * e, (x + 1) * e)
        return xr * kr[:, cols] + xi * ki[:, cols], xi * kr[:, cols] - xr * ki[:, cols]

    a_r, a_i = spectral(er, ei, 0)
    b_r, b_i = spectral(pr, pi, 2)
    acce_ref[...] += _dot(cc_ref[...], (a_r + b_r).astype(BF16)) + _dot(sc_ref[...], (a_i + b_i).astype(BF16))
    a_r, a_i = spectral(er, ei, 1)
    b_r, b_i = spectral(pr, pi, 0)
    acco_ref[...] += _dot(cc_ref[...], (a_r + b_r).astype(BF16)) + _dot(sc_ref[...], (a_i + b_i).astype(BF16))

    @pl.when(f == pl.num_programs(1) - 1)
    def _():
        sign = (1 - 2 * (row & 1)).astype(F32)
        nyq_e = sign * nyq_ref[0:1, :e]
        nyq_o = sign * nyq_ref[0:1, e:]

        def chunk(c, carry):
            rows = pl.ds(pl.multiple_of(c * n, n), n)
            tokens = pl.ds(pl.multiple_of(2 * c * n, 2 * n), 2 * n)
            y_e = (x0e_s[rows, :].astype(F32) * (acce_ref[rows, :] + nyq_e)).astype(BF16)
            y_o = (x0o_s[rows, :].astype(F32) * (acco_ref[rows, :] + nyq_o)).astype(BF16)
            o_ref[tokens, :] = _dot(merge, jnp.concatenate([y_e, y_o], axis=0)).astype(BF16)
            return carry

        lax.fori_loop(0, half // n, chunk, 0, unroll=2)


def _hyena(p, layer, short_w, short_b, skip, tables, spectrum):
    nb, length, _ = p.shape
    half = length // 2
    cos_t, sin_t, sign_rows = tables
    kr, ki, kn = spectrum
    tf = min(512, half)
    col = lambda j: pl.BlockSpec((None, length, E_HY), lambda b, f: (b, 0, j))
    return pl.pallas_call(
        functools.partial(_hyena_kernel, half=half),
        out_shape=jax.ShapeDtypeStruct((nb, length, E_HY), BF16),
        grid=(nb, half // tf),
        in_specs=[
            col(COL_HY_V), col(COL_HY_X0), col(COL_HY_X1),
            pl.BlockSpec((None, 3, 3 * E_HY), lambda b, f: (layer, 0, 0)),
            pl.BlockSpec((None, 1, 3 * E_HY), lambda b, f: (layer, 0, 0)),
            pl.BlockSpec((None, 1, E_HY), lambda b, f: (layer, 0, 0)),
            pl.BlockSpec((SUBLANES, half), lambda b, f: (0, 0)),
            pl.BlockSpec((SUBLANES, 3 * E_HY), lambda b, f: (0, 0)),
            pl.BlockSpec((tf, half), lambda b, f: (f, 0)),
            pl.BlockSpec((tf, half), lambda b, f: (f, 0)),
            pl.BlockSpec((half, tf), lambda b, f: (0, f)),
            pl.BlockSpec((half, tf), lambda b, f: (0, f)),
            pl.BlockSpec((tf, 3 * E_HY), lambda b, f: (f, 0)),
            pl.BlockSpec((tf, 3 * E_HY), lambda b, f: (f, 0)),
        ],
        out_specs=pl.BlockSpec((None, length, E_HY), lambda b, f: (b, 0, 0)),
        scratch_shapes=[pltpu.VMEM((half, E_HY), BF16)] * 4 + [pltpu.VMEM((half, E_HY), F32)] * 2
                       + [pltpu.VMEM((SUBLANES, 2 * E_HY), F32)],
        compiler_params=_cparams("parallel", "arbitrary"),
        name="hyena",
    )(p, p, p, short_w, short_b, skip, sign_rows, kn, cos_t, sin_t, cos_t, sin_t, kr, ki)


LRU_PAD = SUBLANES
LRU_CHUNK = 128


def _lru_kernel(xc_ref, gc_ref, xl_ref, gl_ref, cw_ref, cb_ref, wg_ref, bg_ref, lam_ref, yc_ref, yl_ref,
                xpad, a_cum, b_cum, h_sum, *, tc, tl):
    off_c = LRU_PAD
    off_l = off_c + tc + LRU_PAD
    n = LRU_CHUNK
    nj = E_LRU // LANES
    n_chunks_c = tc // n
    n_chunks = (tc + tl) // n
    groups_c = tc // SUBLANES
    groups = (tc + tl) // SUBLANES

    xpad[...] = jnp.zeros_like(xpad)
    xpad[off_c:off_c + tc, :] = xc_ref[...].astype(F32)
    xpad[off_l:off_l + tl, :] = xl_ref[...].astype(F32)

    sub = lax.broadcasted_iota(jnp.int32, (n // SUBLANES, SUBLANES, LANES), 1)

    for d in range(2):
        lam = lam_ref[d:d + 1, :]
        softplus_neg = jnp.maximum(-lam, 0.0) + jnp.log1p(jnp.exp(-jnp.abs(lam)))

        def phase_a(k, carry, d=d, softplus_neg=softplus_neg):
            start = pl.multiple_of(off_c + k * n + jnp.where(k >= n_chunks_c, LRU_PAD, 0), SUBLANES)
            for j in range(nj):
                cols = slice(j * LANES, (j + 1) * LANES)
                win = xpad[pl.ds(start - LRU_PAD, n + 2 * LRU_PAD), cols]
                xc = cb_ref[d:d + 1, cols] + jnp.zeros((n, LANES), F32)
                for tap in range(LRU_CONV):
                    back = LRU_CONV - 1 - tap
                    shift = back if d == 0 else (n + 2 * LRU_PAD - back) % (n + 2 * LRU_PAD)
                    src = win if back == 0 else pltpu.roll(win, shift, 0)
                    xc = xc + cw_ref[d, tap:tap + 1, cols] * src[LRU_PAD:LRU_PAD + n]
                gates = _dot(xc.astype(BF16), wg_ref[d, j]) + bg_ref[d, j:j + 1, :]
                r = jax.nn.sigmoid(gates[:, :LANES])
                i = jax.nn.sigmoid(gates[:, LANES:])
                log_a = (-LRU_C) * r * softplus_neg[:, cols]
                a = jnp.exp(log_a)
                b = jnp.sqrt(-jnp.tanh(log_a) * (a * a + 1.0)) * (i * xc)
                a = a.reshape(n // SUBLANES, SUBLANES, LANES)
                b = b.reshape(n // SUBLANES, SUBLANES, LANES)
                for s in (1, 2, 4):
                    if d == 0:
                        keep = sub >= s
                        a_prev = pltpu.roll(a, s, 1)
                        b_prev = pltpu.roll(b, s, 1)
                    else:
                        keep = sub < SUBLANES - s
                        a_prev = pltpu.roll(a, SUBLANES - s, 1)
                        b_prev = pltpu.roll(b, SUBLANES - s, 1)
                    b = jnp.where(keep, a * b_prev + b, b)
                    a = jnp.where(keep, a * a_prev, a)
                a_cum[pl.ds(start, n), cols] = a.reshape(n, LANES)
                b_cum[pl.ds(start, n), cols] = b.reshape(n, LANES)
            return carry

        lax.fori_loop(0, n_chunks, phase_a, 0)

        def phase_b(g, h_prev, d=d):
            if d == 0:
                row = off_c + g * SUBLANES + jnp.where(g >= groups_c, LRU_PAD, 0)
            else:
                row = jnp.where(g < groups_c, off_c + (groups_c - 1 - g) * SUBLANES,
                                off_l + (groups - 1 - g) * SUBLANES)
            rows = pl.ds(pl.multiple_of(row, SUBLANES), SUBLANES)
            h = a_cum[rows, :] * h_prev + b_cum[rows, :]
            if d == 0:
                h_sum[rows, :] = h
                return jnp.broadcast_to(h[SUBLANES - 1:SUBLANES, :], (SUBLANES, E_LRU))
            h_sum[rows, :] = h_sum[rows, :] + h
            return jnp.broadcast_to(h[0:1, :], (SUBLANES, E_LRU))

        lax.fori_loop(0, groups, phase_b, jnp.zeros((SUBLANES, E_LRU), F32), unroll=4)

    yc_ref[...] = (h_sum[off_c:off_c + tc, :] * _gelu_tanh(gc_ref[...].astype(F32))).astype(BF16)

    def phase_c(k, carry):
        r0 = pl.multiple_of(k * 256, 256)
        yl_ref[pl.ds(r0, 256), :] = (h_sum[pl.ds(off_l + r0, 256), :]
                                     * _gelu_tanh(gl_ref[pl.ds(r0, 256), :].astype(F32))).astype(BF16)
        return carry

    lax.fori_loop(0, tl // 256, phase_c, 0)


def _lru_gate_weights(wr, wi, br, bi):
    per = LANES // LRU_BLOCK_DIM
    nj = E_LRU // LANES

    def lane_blocks(w):
        w = w.reshape(DEPTH, 2, nj, per, LRU_BLOCK_DIM, LRU_BLOCK_DIM)
        eye = jnp.eye(per, dtype=w.dtype)
        full = w[:, :, :, :, :, None, :] * eye[None, None, None, :, None, :, None]
        return full.reshape(DEPTH, 2, nj, LANES, LANES)

    wg = jnp.concatenate([lane_blocks(wr), lane_blocks(wi)], axis=-1).astype(BF16)
    bg = jnp.concatenate([br.reshape(DEPTH, 2, nj, LANES), bi.reshape(DEPTH, 2, nj, LANES)], axis=-1)
    return wg, bg


def _lru(p_ctx, p_lat, layer, conv_w, conv_b, wg, bg, lam):
    nb, tc, _ = p_ctx.shape
    tl = p_lat.shape[1]
    rows = tc + tl + 3 * LRU_PAD
    col = lambda t, j: pl.BlockSpec((None, t, E_LRU), lambda b: (b, 0, j))
    par = lambda shape: pl.BlockSpec((None,) + shape, lambda b: (layer,) + (0,) * len(shape))
    return pl.pallas_call(
        functools.partial(_lru_kernel, tc=tc, tl=tl),
        out_shape=[jax.ShapeDtypeStruct((nb, tc, E_LRU), BF16), jax.ShapeDtypeStruct((nb, tl, E_LRU), BF16)],
        grid=(nb,),
        in_specs=[col(tc, COL_LRU_X), col(tc, COL_LRU_G), col(tl, COL_LRU_X), col(tl, COL_LRU_G),
                  par((2, LRU_CONV, E_LRU)), par((2, E_LRU)), par((2, E_LRU // LANES, LANES, 2 * LANES)),
                  par((2, E_LRU // LANES, 2 * LANES)), par((2, E_LRU))],
        out_specs=[pl.BlockSpec((None, tc, E_LRU), lambda b: (b, 0, 0)),
                   pl.BlockSpec((None, tl, E_LRU), lambda b: (b, 0, 0))],
        scratch_shapes=[pltpu.VMEM((rows, E_LRU), F32)] * 4,
        compiler_params=_cparams("parallel"),
        name="lru",
    )(p_ctx, p_ctx, p_lat, p_lat, conv_w, conv_b, wg, bg, lam)


def _cumsum_chunk(x, sub, reverse):
    groups = x.shape[0] // SUBLANES
    x3 = x.reshape(groups, SUBLANES, x.shape[1])
    for s in (1, 2, 4):
        if reverse:
            x3 = x3 + jnp.where(sub < SUBLANES - s, pltpu.roll(x3, SUBLANES - s, 1), 0.0)
        else:
            x3 = x3 + jnp.where(sub >= s, pltpu.roll(x3, s, 1), 0.0)
    edge = 0 if reverse else SUBLANES - 1
    blocks = [None] * groups
    total = None
    for g in (range(groups - 1, -1, -1) if reverse else range(groups)):
        blk = x3[g] if total is None else x3[g] + total
        blocks[g] = blk
        total = blk[edge:edge + 1, :]
    return blocks


def _hgrn_kernel(qc_ref, ffc_ref, fbc_ref, ic_ref, ogc_ref, ql_ref, ffl_ref, fbl_ref, il_ref, ogl_ref,
                 lb_ref, ng_ref, yc_ref, yl_ref, o_acc, o_f, o_b, qg_s, kd_s, v_s, dec_s, *, layer, tc, tl):
    ch = HG_CHUNK
    raw = [lb_ref[i:i + 1, :] for i in range(DEPTH)]
    top = functools.reduce(jnp.maximum, raw)
    ex = [jnp.exp(r - top) for r in raw]
    tot = functools.reduce(lambda a, b: a + b, ex)
    lb = jnp.zeros((1, E_HG), F32)
    for i in range(1, layer + 1):
        lb = lb + ex[i] / tot

    rt = lax.broadcasted_iota(jnp.int32, (ch, ch), 0)
    ct = lax.broadcasted_iota(jnp.int32, (ch, ch), 1)
    tri = [(rt >= ct), (rt <= ct)]
    sub = lax.broadcasted_iota(jnp.int32, (ch // SUBLANES, SUBLANES, HG_DK), 1)
    n_c, n_l = tc // ch, tl // ch
    n_all = n_c + n_l
    mid_g = (ch // 2) // SUBLANES
    last = SUBLANES - 1

    def intra(q_ref, ff_ref, fb_ref, i_ref, n_chunks, chunk0):
        def body(it, carry):
            chains = []
            for c in (it * HG_TRIP + i for i in range(HG_TRIP)):
                rows = pl.ds(pl.multiple_of(c * ch, ch), ch)
                srows = pl.ds(pl.multiple_of((chunk0 + c) * ch, ch), ch)
                drows = pl.ds(pl.multiple_of((chunk0 + c) * SUBLANES, SUBLANES), SUBLANES)
                for hd in range(HG_HEADS):
                    cols = slice(hd * HG_DK, (hd + 1) * HG_DK)
                    lbh = lb[:, cols]
                    q = _silu(q_ref[rows, cols].astype(F32))
                    v = i_ref[rows, cols]
                    v_s[srows, cols] = v
                    for d, f_ref in enumerate((ff_ref, fb_ref)):
                        f = lbh + (1.0 - lbh) * jax.nn.sigmoid(f_ref[rows, cols].astype(F32))
                        k = 1.0 - f
                        blocks = _cumsum_chunk(jnp.log(f), sub, reverse=d == 1)
                        b = jnp.concatenate(blocks, axis=0)
                        if d == 0:
                            b_last, b_mid = blocks[-1][last:last + 1], blocks[mid_g][0:1]
                        else:
                            b_last, b_mid = blocks[0][0:1], blocks[mid_g - 1][last:last + 1]
                        e_mid = jnp.exp(b - b_mid)
                        qs = (q * e_mid).astype(BF16)
                        ks = (k * jnp.exp(b_mid - b)).astype(BF16)
                        qg_s[d, srows, cols] = (q * e_mid * jnp.exp(b_mid)).astype(BF16)
                        kd_s[d, srows, cols] = (k * jnp.exp(b_last - b)).astype(BF16)
                        dec_s[d, drows, cols] = jnp.broadcast_to(jnp.exp(b_last), (SUBLANES, HG_DK))
                        chains.append((srows, cols, d, qs, ks, v))
            scores = [_dot_nt(qs, ks) for _, _, _, qs, ks, _ in chains]
            masked = [jnp.where(tri[chain[2]], s, 0.0).astype(BF16) for chain, s in zip(chains, scores)]
            outs = [_dot(s, chain[5]) for chain, s in zip(chains, masked)]
            for i in range(0, len(chains), 2):
                srows, cols = chains[i][:2]
                o_acc[srows, cols] = outs[i] + outs[i + 1]
            return carry

        lax.fori_loop(0, n_chunks // HG_TRIP, body, 0)

    intra(qc_ref, ffc_ref, fbc_ref, ic_ref, n_c, 0)
    intra(ql_ref, ffl_ref, fbl_ref, il_ref, n_l, n_c)

    def inter(it, states):
        new = []
        for d in range(2):
            g = it if d == 0 else jnp.where(it < n_c, n_c - 1 - it, n_all + n_c - 1 - it)
            rows = pl.ds(pl.multiple_of(g * ch, ch), ch)
            drows = pl.ds(pl.multiple_of(g * SUBLANES, SUBLANES), SUBLANES)
            o_d = o_f if d == 0 else o_b
            for hd in range(HG_HEADS):
                cols = slice(hd * HG_DK, (hd + 1) * HG_DK)
                s_t = states[d * HG_HEADS + hd]
                o_d[rows, cols] = _dot_nt(qg_s[d, rows, cols], s_t.astype(BF16))
                dec = dec_s[d, drows, cols][0:1]
                new.append(s_t * dec + _dot_tn(v_s[rows, cols], kd_s[d, rows, cols]))
        return tuple(new)

    lax.fori_loop(0, n_all, inter, tuple(jnp.zeros((HG_DV, HG_DK), F32) for _ in range(2 * HG_HEADS)), unroll=4)

    def finish(og_ref, y_ref, base, t):
        blk = min(t, 256)

        def body(k, carry):
            r0 = pl.multiple_of(k * blk, blk)
            for hd in range(HG_HEADS):
                cols = slice(hd * HG_DV, (hd + 1) * HG_DV)
                orow = pl.ds(base + r0, blk)
                o = o_acc[orow, cols] + o_f[orow, cols] + o_b[orow, cols]
                y = _rms(o) * ng_ref[...] * _silu(og_ref[pl.ds(r0, blk), cols].astype(F32))
                y_ref[pl.ds(r0, blk), cols] = y.astype(BF16)
            return carry

        lax.fori_loop(0, t // blk, body, 0)

    finish(ogc_ref, yc_ref, 0, tc)
    finish(ogl_ref, yl_ref, tc, tl)


def _hgrn(p_ctx, p_lat, layer, lower_bounds, norm_g):
    nb, tc, _ = p_ctx.shape
    tl = p_lat.shape[1]
    t = tc + tl
    col = lambda n, j: pl.BlockSpec((None, n, E_HG), lambda b: (b, 0, j), pipeline_mode=pl.Buffered(1))
    names = (COL_HG_Q, COL_HG_FF, COL_HG_FB, COL_HG_I, COL_HG_OG)
    return pl.pallas_call(
        functools.partial(_hgrn_kernel, layer=layer, tc=tc, tl=tl),
        out_shape=[jax.ShapeDtypeStruct((nb, tc, E_HG), BF16), jax.ShapeDtypeStruct((nb, tl, E_HG), BF16)],
        grid=(nb,),
        in_specs=[col(tc, j) for j in names] + [col(tl, j) for j in names] + [
            pl.BlockSpec((DEPTH, E_HG), lambda b: (0, 0)),
            pl.BlockSpec((None, 1, HG_DV), lambda b: (layer, 0, 0)),
        ],
        out_specs=[pl.BlockSpec((None, tc, E_HG), lambda b: (b, 0, 0)),
                   pl.BlockSpec((None, tl, E_HG), lambda b: (b, 0, 0))],
        scratch_shapes=[pltpu.VMEM((t, E_HG), F32), pltpu.VMEM((t, E_HG), F32), pltpu.VMEM((t, E_HG), F32),
                        pltpu.VMEM((2, t, E_HG), BF16), pltpu.VMEM((2, t, E_HG), BF16), pltpu.VMEM((t, E_HG), BF16),
                        pltpu.VMEM((2, t // HG_CHUNK * SUBLANES, E_HG), F32)],
        compiler_params=_cparams("parallel"),
        name="hgrn",
    )(*([p_ctx] * 5 + [p_lat] * 5 + [lower_bounds, norm_g]))


def _merge_kernel(gh_ref, gl_ref, gg_ref, yh_ref, yl_ref, yg_ref, wh_ref, wl_ref, wg_ref, wo_ref,
                  h_ref, g1_ref, gain_ref, o_ref, *t_ref, n_cols, n_rows):
    def branch(g_ref, y_ref, w_ref):
        return jax.nn.sigmoid(g_ref[...].astype(F32)) * _dot(y_ref[...], w_ref[...])

    m = branch(gh_ref, yh_ref, wh_ref) + branch(gl_ref, yl_ref, wl_ref) + branch(gg_ref, yg_ref, wg_ref)
    r = _rms(_dot(m.astype(BF16), wo_ref[...])) * (gain_ref[...] * g1_ref[...])
    if n_cols:
        for j in range(D_MODEL // LANES):
            lanes = slice(j * LANES, (j + 1) * LANES)
            for c in range(n_cols):
                t_ref[0][j, pl.ds(c, n_rows, stride=n_cols), :] = r[c * n_rows:(c + 1) * n_rows, lanes]
            o_ref[:, :, lanes] = h_ref[:, :, lanes] + t_ref[0][j].reshape(n_rows, n_cols, LANES)
    else:
        o_ref[...] = h_ref[...] + r


def _merge(p, y_hy, y_lru, y_hg, w_hy, w_lru, w_hg, w_out, h, ada4, gains4, layer, *, ctx_row, col_major):
    nb, t, _ = h.shape
    tm = 512
    row_fn = (lambda b: b) if ctx_row is None else (lambda b: ctx_row)
    if col_major:
        n_rows = t // GRID_W
        n_cols = tm // n_rows
        h = h.reshape(nb, n_rows, GRID_W, D_MODEL)
        h_spec = pl.BlockSpec((None, n_rows, n_cols, D_MODEL), lambda b, i: (b, 0, i, 0))
        scratch = [pltpu.VMEM((D_MODEL // LANES, tm, LANES), F32)]
    else:
        n_rows = n_cols = 0
        h_spec = pl.BlockSpec((None, tm, D_MODEL), lambda b, i: (b, i, 0))
        scratch = []
    gate = lambda k: pl.BlockSpec((None, tm, D_MODEL), lambda b, i: (b, i, COL_GATES + k))
    y_spec = pl.BlockSpec((None, tm, E_HY), lambda b, i: (b, i, 0))
    w_spec = pl.BlockSpec((None, E_HY, D_MODEL), lambda b, i: (layer, 0, 0))
    out = pl.pallas_call(
        functools.partial(_merge_kernel, n_cols=n_cols, n_rows=n_rows),
        out_shape=jax.ShapeDtypeStruct(h.shape, F32),
        grid=(nb, t // tm),
        in_specs=[gate(0), gate(1), gate(2), y_spec, y_spec, y_spec, w_spec, w_spec, w_spec,
                  pl.BlockSpec((None, D_MODEL, D_MODEL), lambda b, i: (layer, 0, 0)),
                  h_spec, _ada_spec(layer, row_fn, 2, 2), _gain_spec(layer, 1, 2)],
        out_specs=h_spec,
        scratch_shapes=scratch,
        compiler_params=_cparams("parallel", "parallel"),
        name="merge",
    )(p, p, p, y_hy, y_lru, y_hg, w_hy, w_lru, w_hg, w_out, h, ada4, gains4)
    return out.reshape(nb, t, D_MODEL)


def _mlp_kernel(h_ref, sh_ref, sc_ref, g2_ref, gain2_ref, gain3_ref, w1_ref, w2_ref, o_ref):
    h = h_ref[...]
    u = (_rms(h) * (gain2_ref[...] * (1.0 + sc_ref[...])) + sh_ref[...]).astype(BF16)
    acc = None
    for k in range(0, D_FF, MLP_FF_BLOCK):
        a = jnp.maximum(_dot(u, w1_ref[:, k:k + MLP_FF_BLOCK]), 0.0)
        part = _dot((a * a).astype(BF16), w2_ref[k:k + MLP_FF_BLOCK, :])
        acc = part if acc is None else acc + part
    o_ref[...] = h + (g2_ref[...] * gain3_ref[...]) * _rms(acc)


def _mlp(h, ada4, gains4, w1_b, w2_b, layer, *, ctx_row):
    nb, t, _ = h.shape
    tm = 512
    row_fn = (lambda b: b) if ctx_row is None else (lambda b: ctx_row)
    h_spec = pl.BlockSpec((None, tm, D_MODEL), lambda b, i: (b, i, 0))
    return pl.pallas_call(
        _mlp_kernel,
        out_shape=jax.ShapeDtypeStruct(h.shape, F32),
        grid=(nb, t // tm),
        in_specs=[h_spec, _ada_spec(layer, row_fn, 3, 2), _ada_spec(layer, row_fn, 4, 2),
                  _ada_spec(layer, row_fn, 5, 2), _gain_spec(layer, 2, 2), _gain_spec(layer, 3, 2),
                  pl.BlockSpec((None, D_MODEL, D_FF), lambda b, i: (layer, 0, 0), pipeline_mode=pl.Buffered(1)),
                  pl.BlockSpec((None, D_FF, D_MODEL), lambda b, i: (layer, 0, 0), pipeline_mode=pl.Buffered(1))],
        out_specs=h_spec,
        compiler_params=_cparams("parallel", "parallel"),
        name="mlp",
    )(h, ada4, ada4, ada4, gains4, gains4, w1_b, w2_b)


def kernel(x, c, ctx, c_ctx, w_ada, b_ada, norm_gains, w_in, hy_short_w, hy_short_b, hy_ff_w1, hy_ff_b1,
           hy_ff_w2, hy_ff_b2, hy_ff_w3, hy_freq, hy_skip, lru_conv_w, lru_conv_b, lru_wr, lru_br, lru_wi,
           lru_bi, lru_lambda, hg_lower_bounds, hg_norm_g, w_proj_hy, w_proj_lru, w_proj_hg, w_out,
           w_mlp1, w_mlp2):
    nb, seq, _ = x.shape
    ctx_len = ctx.shape[1]
    assert seq % GRID_W == 0

    w_in_b, w_out_b = w_in.astype(BF16), w_out.astype(BF16)
    w_hy_b, w_lru_b, w_hg_b = w_proj_hy.astype(BF16), w_proj_lru.astype(BF16), w_proj_hg.astype(BF16)
    w1_b, w2_b = w_mlp1.astype(BF16), w_mlp2.astype(BF16)

    ctx_row = nb
    ada_rows = -(-(nb + 1) // SUBLANES) * SUBLANES
    cin = jnp.concatenate([c, c_ctx[None], jnp.zeros((ada_rows - nb - 1, D_MODEL), F32)], axis=0)
    ada4 = _ada_call(cin, w_ada, b_ada).reshape(DEPTH, ada_rows, 1, N_ADA * D_MODEL)
    gains4 = norm_gains.reshape(DEPTH, 4, 1, D_MODEL)

    hy_w1p = jnp.pad(hy_ff_w1, ((0, 0), (0, LANES - hy_ff_w1.shape[1]), (0, 0)))
    hy_b1 = hy_ff_b1.reshape(DEPTH, 1, HY_ORDER)
    hy_b2 = hy_ff_b2.reshape(DEPTH, 1, HY_ORDER)
    hy_sb = hy_short_b.reshape(DEPTH, 1, 3 * E_HY)
    hy_sk = hy_skip.reshape(DEPTH, 1, E_HY)
    lru_wg, lru_bg = _lru_gate_weights(lru_wr, lru_wi, lru_br, lru_bi)
    hg_ng = hg_norm_g.reshape(DEPTH, 1, HG_DV)

    tables = {seq // 2: _dft_tables(seq // 2), ctx_len // 2: _dft_tables(ctx_len // 2)}

    def hyena(p, layer):
        length = p.shape[1]
        hs, hd = _hy_taps(length, layer, hy_w1p, hy_b1, hy_ff_w2, hy_b2, hy_ff_w3, hy_freq)
        spectrum = _hy_spectrum(length // 2, *tables[length // 2], hs, hd)
        return _hyena(p, layer, hy_short_w, hy_sb, hy_sk, tables[length // 2], spectrum)

    h_lat = x
    h_ctx = ctx.reshape(1, nb * ctx_len, D_MODEL)
    for layer in range(DEPTH):
        need_ctx = layer < DEPTH - 1
        col_major = layer % 2 == 1
        p_lat = _in_proj(h_lat, ada4, gains4, w_in_b, layer, ctx_row=None, col_major=col_major)
        p_ctx_flat = _in_proj(h_ctx, ada4, gains4, w_in_b, layer, ctx_row=ctx_row, col_major=False)
        p_ctx = p_ctx_flat.reshape(nb, ctx_len, IN_WIDTH)

        y_hy_lat = hyena(p_lat, layer)
        y_lru_ctx, y_lru_lat = _lru(p_ctx, p_lat, layer, lru_conv_w, lru_conv_b, lru_wg, lru_bg, lru_lambda)
        y_hg_ctx, y_hg_lat = _hgrn(p_ctx, p_lat, layer, hg_lower_bounds, hg_ng)
        proj = (w_hy_b, w_lru_b, w_hg_b, w_out_b)
        h_lat = _merge(p_lat, y_hy_lat, y_lru_lat, y_hg_lat, *proj, h_lat, ada4, gains4, layer,
                       ctx_row=None, col_major=col_major)
        if need_ctx:
            flat = lambda y: y.reshape(1, nb * ctx_len, y.shape[-1])
            y_hy_ctx = hyena(p_ctx, layer)
            h_ctx = _merge(p_ctx_flat, flat(y_hy_ctx), flat(y_lru_ctx), flat(y_hg_ctx), *proj, h_ctx, ada4,
                           gains4, layer, ctx_row=ctx_row, col_major=False)

        h_lat = _mlp(h_lat, ada4, gains4, w1_b, w2_b, layer, ctx_row=None)
        if need_ctx:
            h_ctx = _mlp(h_ctx, ada4, gains4, w1_b, w2_b, layer, ctx_row=ctx_row)
    return h_lat
```

```python
import functools
import math

import jax
import jax.numpy as jnp
from jax import lax
from jax.experimental import pallas as pl
from jax.experimental.pallas import tpu as pltpu

F32 = jnp.float32
BF16 = jnp.bfloat16
HIGHEST = lax.Precision.HIGHEST

D_MODEL = 1024
DEPTH = 2
GRID_W = 64
EPS = 1e-6

E_HY = 512
HY_EMB_BANDS = 8
HY_ORDER = 64
HY_DECAY_TARGET = 1e-2
HY_FAST_DECAY = 0.3
HY_SLOW_DECAY = 1.5
HY_ROWS = 64
DFT_COARSE = 64

E_LRU = 512
LRU_BLOCKS = 8
LRU_BLOCK_DIM = E_LRU // LRU_BLOCKS
LRU_CONV = 4
LRU_C = 8.0

HG_HEADS = 4
HG_DK = 128
HG_DV = 128
E_HG = HG_HEADS * HG_DV
HG_CHUNK = 64
HG_TRIP = 4

D_FF = 4 * D_MODEL
IN_WIDTH = 8 * D_MODEL
N_ADA = 6

COL_HY_V, COL_HY_X0, COL_HY_X1 = 0, 1, 2
COL_LRU_X, COL_LRU_G = 3, 4
COL_HG_Q, COL_HG_FF, COL_HG_FB, COL_HG_I, COL_HG_OG = 5, 6, 7, 8, 9
COL_GATES = 5

LANES = 128
SUBLANES = 8
VMEM_LIMIT = 56 * 1024 * 1024
NORM_ROWS = 512
MLP_FF_BLOCK = 1024


def _cparams(*sem):
    return pltpu.CompilerParams(dimension_semantics=sem, vmem_limit_bytes=VMEM_LIMIT)


def _rms(x):
    return x * lax.rsqrt(jnp.mean(x * x, axis=-1, keepdims=True) + EPS)


def _silu(x):
    return x * jax.nn.sigmoid(x)


def _gelu_tanh(x):
    return x * (0.5 * (1.0 + jnp.tanh(math.sqrt(2.0 / math.pi) * (x + 0.044715 * (x * x * x)))))


def _dot(a, b):
    return jnp.dot(a, b, preferred_element_type=F32)


def _dot_nt(a, b):
    return lax.dot_general(a, b, (((1,), (1,)), ((), ())), preferred_element_type=F32)


def _dot_tn(a, b):
    return lax.dot_general(a, b, (((0,), (0,)), ((), ())), preferred_element_type=F32)


def _ada_spec(layer, row_fn, chunk, nargs):
    if nargs == 2:
        return pl.BlockSpec((None, None, 1, D_MODEL), lambda b, i: (layer, row_fn(b), 0, chunk))
    return pl.BlockSpec((None, None, 1, D_MODEL), lambda b, i, j: (layer, row_fn(b), 0, chunk))


def _gain_spec(layer, idx, nargs):
    if nargs == 2:
        return pl.BlockSpec((None, None, 1, D_MODEL), lambda b, i: (layer, idx, 0, 0))
    return pl.BlockSpec((None, None, 1, D_MODEL), lambda b, i, j: (layer, idx, 0, 0))


def _ada_kernel(c_ref, w_ref, b_ref, o_ref):
    s = _silu(c_ref[...])
    o_ref[...] = _dot(s.astype(BF16), w_ref[...].astype(BF16)) + b_ref[...]


def _ada_call(cin, w_ada_b, b_ada):
    tn = D_MODEL
    rows = cin.shape[0]
    return pl.pallas_call(
        _ada_kernel,
        out_shape=jax.ShapeDtypeStruct((DEPTH, rows, N_ADA * D_MODEL), F32),
        grid=(DEPTH, N_ADA * D_MODEL // tn),
        in_specs=[
            pl.BlockSpec((rows, D_MODEL), lambda l, j: (0, 0)),
            pl.BlockSpec((None, D_MODEL, tn), lambda l, j: (l, 0, j)),
            pl.BlockSpec((None, 1, tn), lambda l, j: (l, 0, j)),
        ],
        out_specs=pl.BlockSpec((None, rows, tn), lambda l, j: (l, 0, j)),
        compiler_params=_cparams("parallel", "parallel"),
        name="ada",
    )(cin, w_ada_b, b_ada.reshape(DEPTH, 1, N_ADA * D_MODEL))


def _inproj_kernel(h_ref, sh_ref, sc_ref, g_ref, w_ref, o_ref, u_ref, *t_ref, n_cols, n_rows):
    @pl.when(pl.program_id(2) == 0)
    def _():
        mul = g_ref[...] * (1.0 + sc_ref[...])
        sh = sh_ref[...]
        if n_cols:
            group = SUBLANES * n_rows
            out_row = lax.broadcasted_iota(jnp.int32, (group, group), 0)
            in_row = lax.broadcasted_iota(jnp.int32, (group, group), 1)
            reorder = jnp.where(in_row == (out_row % n_rows) * SUBLANES + out_row // n_rows, 1.0, 0.0).astype(BF16)
            for g in range(n_cols // SUBLANES):
                x = h_ref[:, g * SUBLANES:(g + 1) * SUBLANES, :].reshape(group, D_MODEL)
                y = (_rms(x) * mul + sh).astype(BF16)
                u_ref[g * group:(g + 1) * group, :] = _dot(reorder, y).astype(BF16)
        else:
            for k in range(0, u_ref.shape[0], NORM_ROWS):
                u_ref[k:k + NORM_ROWS, :] = (_rms(h_ref[k:k + NORM_ROWS, :]) * mul + sh).astype(BF16)

    o_ref[...] = _dot(u_ref[...], w_ref[...]).astype(BF16)


def _in_proj(h, ada4, gains4, w_in_b, layer, *, ctx_row, col_major):
    nb, t, _ = h.shape
    tm, tn = min(2048, t), 1024
    row_fn = (lambda b: b) if ctx_row is None else (lambda b: ctx_row)
    if col_major:
        n_rows = t // GRID_W
        n_cols = tm // n_rows
        h = h.reshape(nb, n_rows, GRID_W, D_MODEL)
        h_spec = pl.BlockSpec((None, n_rows, n_cols, D_MODEL), lambda b, i, j: (b, 0, i, 0))
        scratch = [pltpu.VMEM((tm, D_MODEL), BF16)]
    else:
        n_rows = n_cols = 0
        h_spec = pl.BlockSpec((None, tm, D_MODEL), lambda b, i, j: (b, i, 0))
        scratch = [pltpu.VMEM((tm, D_MODEL), BF16)]
    return pl.pallas_call(
        functools.partial(_inproj_kernel, n_cols=n_cols, n_rows=n_rows),
        out_shape=jax.ShapeDtypeStruct((nb, t, IN_WIDTH), BF16),
        grid=(nb, t // tm, IN_WIDTH // tn),
        in_specs=[
            h_spec,
            _ada_spec(layer, row_fn, 0, 3),
            _ada_spec(layer, row_fn, 1, 3),
            _gain_spec(layer, 0, 3),
            pl.BlockSpec((None, D_MODEL, tn), lambda b, i, j: (layer, 0, j)),
        ],
        out_specs=pl.BlockSpec((None, tm, tn), lambda b, i, j: (b, i, j)),
        scratch_shapes=scratch,
        compiler_params=_cparams("parallel", "parallel", "arbitrary"),
        name="in_proj",
    )(h, ada4, ada4, gains4, w_in_b)


def _dft_tables(length):
    n = 2 * length
    idx = jnp.arange(length, dtype=jnp.int32)

    def trig(freqs):
        ang = ((freqs[:, None] * idx[None, :]) % n).astype(F32) * (2.0 * math.pi / n)
        return jnp.cos(ang), jnp.sin(ang)

    c_hi, s_hi = trig(jnp.arange(length // DFT_COARSE, dtype=jnp.int32) * DFT_COARSE)
    c_lo, s_lo = trig(jnp.arange(DFT_COARSE, dtype=jnp.int32))
    cos_t = (c_hi[:, None] * c_lo[None] - s_hi[:, None] * s_lo[None]).reshape(length, length)
    sin_t = (s_hi[:, None] * c_lo[None] + c_hi[:, None] * s_lo[None]).reshape(length, length)
    sign = (1 - 2 * (idx % 2)).astype(F32)
    sign_rows = jnp.zeros((SUBLANES, length), F32).at[0].set(sign)
    return cos_t.astype(BF16), sin_t.astype(BF16), sign_rows.astype(BF16)


def _hy_taps_kernel(w1_ref, b1_ref, w2_ref, b2_ref, w3_ref, fr_ref, hs_ref, hd_ref, *, length, tl):
    row0 = pl.program_id(0) * tl
    m = lax.broadcasted_iota(jnp.int32, (tl, LANES), 0) + row0
    m_wide = lax.broadcasted_iota(jnp.int32, (tl, E_HY), 0) + row0
    lane = lax.broadcasted_iota(jnp.int32, (tl, LANES), 1)
    band_step = (HY_EMB_BANDS - 1 - 1e-4) / (HY_EMB_BANDS - 1)
    band = 1e-4 + jnp.where(lane <= HY_EMB_BANDS, lane - 1, lane - 1 - HY_EMB_BANDS).astype(F32) * band_step
    ch = lax.broadcasted_iota(jnp.int32, (1, E_HY), 1).astype(F32)
    lo = math.log(HY_DECAY_TARGET) / HY_SLOW_DECAY
    hi = math.log(HY_DECAY_TARGET) / HY_FAST_DECAY
    delta = jnp.abs(lo + ch * ((hi - lo) / (E_HY - 1)))

    def taps(offset):
        pos = (2 * m + offset).astype(F32)
        t = pos / max(length - 1, 1)
        ang = band * (2.0 * math.pi * pos / length)
        z = jnp.where(lane == 0, t,
                      jnp.where(lane <= HY_EMB_BANDS, jnp.cos(ang),
                                jnp.where(lane <= 2 * HY_EMB_BANDS, -jnp.sin(ang), 0.0)))
        h = jnp.sin(fr_ref[0:1, :] * (jnp.dot(z, w1_ref[...], precision=HIGHEST, preferred_element_type=F32)
                                      + b1_ref[...]))
        h = jnp.sin(fr_ref[1:2, :] * (jnp.dot(h, w2_ref[...], precision=HIGHEST, preferred_element_type=F32)
                                      + b2_ref[...]))
        h = jnp.dot(h, w3_ref[...], precision=HIGHEST, preferred_element_type=F32)
        decay = jnp.exp(-((2 * m_wide + offset).astype(F32) / max(length - 1, 1)) * delta)
        return h[:, :E_HY] * decay, h[:, E_HY:] * decay

    f_even, b_even = taps(0)
    f_odd, b_odd = taps(1)
    f_prev, b_prev = taps(-1)
    first = m_wide == 0
    forward = (f_even, f_odd, jnp.where(first, b_odd, f_prev))
    backward = (b_even, b_prev, b_odd)
    for x in range(3):
        cols = slice(x * E_HY, (x + 1) * E_HY)
        bwd = jnp.where(first, 0.0, backward[x])
        hs_ref[:, cols] = (forward[x] + bwd).astype(BF16)
        hd_ref[:, cols] = (bwd - forward[x]).astype(BF16)


def _hy_taps(length, layer, w1p, b1, w2, b2, w3, freq):
    half = length // 2
    tl = min(256, half)
    full = lambda shape: pl.BlockSpec((None,) + shape, lambda i: (layer,) + (0,) * len(shape))
    return pl.pallas_call(
        functools.partial(_hy_taps_kernel, length=length, tl=tl),
        out_shape=[jax.ShapeDtypeStruct((half, 3 * E_HY), BF16)] * 2,
        grid=(half // tl,),
        in_specs=[full((LANES, HY_ORDER)), full((1, HY_ORDER)), full((HY_ORDER, HY_ORDER)), full((1, HY_ORDER)),
                  full((HY_ORDER, 2 * E_HY)), full((2, HY_ORDER))],
        out_specs=[pl.BlockSpec((tl, 3 * E_HY), lambda i: (i, 0))] * 2,
        compiler_params=_cparams("parallel"),
        name="hy_taps",
    )(w1p, b1, w2, b2, w3, freq)


def _hy_spec_kernel(c_ref, s_ref, sg_ref, hs_ref, hd_ref, kr_ref, ki_ref, kn_ref, *, length, tf):
    n = 2 * length
    row = lax.broadcasted_iota(jnp.int32, kr_ref.shape, 0) + pl.program_id(0) * tf
    scale = jnp.where(row == 0, 1.0 / n, 2.0 / n)
    kr_ref[...] = _dot(c_ref[...], hs_ref[...]) * scale
    ki_ref[...] = _dot(s_ref[...], hd_ref[...]) * scale
    kn_ref[...] = _dot(sg_ref[...], hs_ref[...]) * (1.0 / n)


def _hy_spectrum(length, cos_t, sin_t, sign_rows, hs, hd):
    tf = min(256, length)
    width = hs.shape[1]
    return pl.pallas_call(
        functools.partial(_hy_spec_kernel, length=length, tf=tf),
        out_shape=[jax.ShapeDtypeStruct((length, width), F32), jax.ShapeDtypeStruct((length, width), F32),
                   jax.ShapeDtypeStruct((SUBLANES, width), F32)],
        grid=(length // tf,),
        in_specs=[
            pl.BlockSpec((tf, length), lambda i: (i, 0)),
            pl.BlockSpec((tf, length), lambda i: (i, 0)),
            pl.BlockSpec((SUBLANES, length), lambda i: (0, 0)),
            pl.BlockSpec((length, width), lambda i: (0, 0)),
            pl.BlockSpec((length, width), lambda i: (0, 0)),
        ],
        out_specs=[pl.BlockSpec((tf, width), lambda i: (i, 0)), pl.BlockSpec((tf, width), lambda i: (i, 0)),
                   pl.BlockSpec((SUBLANES, width), lambda i: (0, 0))],
        compiler_params=_cparams("arbitrary"),
        name="hy_spectrum",
    )(cos_t, sin_t, sign_rows, hs, hd)


def _hyena_kernel(v_ref, x0_ref, x1_ref, sw_ref, sb_ref, skip_ref, sg_ref, kn_ref,
                  cr_ref, sr_ref, cc_ref, sc_ref, kr_ref, ki_ref, o_ref,
                  ue_ref, uo_ref, x0e_s, x0o_s, acce_ref, acco_ref, nyq_ref, *, half):
    f = pl.program_id(1)
    e = E_HY
    n = min(HY_ROWS, half)
    halo = 2 * SUBLANES
    row = lax.broadcasted_iota(jnp.int32, (n, e), 0)
    pick_r = lax.broadcasted_iota(jnp.int32, (2 * n, 2 * n), 0)
    pick_c = lax.broadcasted_iota(jnp.int32, (2 * n, 2 * n), 1)
    split = jnp.where(pick_c == jnp.where(pick_r < n, 2 * pick_r, 2 * (pick_r - n) + 1), 1.0, 0.0).astype(BF16)
    merge = jnp.where(pick_r == jnp.where(pick_c < n, 2 * pick_c, 2 * (pick_c - n) + 1), 1.0, 0.0).astype(BF16)

    @pl.when(f == 0)
    def _():
        def chunk(c, carry):
            r0 = pl.multiple_of(c * n, n)
            rows = pl.ds(r0, n)
            tokens = pl.ds(pl.multiple_of(2 * r0, 2 * n), 2 * n)
            before = pl.ds(pl.multiple_of(jnp.maximum(2 * r0 - halo, 0), halo), halo)
            after = pl.ds(pl.multiple_of(jnp.minimum(2 * r0 + 2 * n, 2 * half - halo), halo), halo)

            def short_conv(ref, j):
                cols = slice(j * e, (j + 1) * e)
                phases = _dot(split, ref[tokens, :])
                xe, xo = phases[:n], phases[n:]
                odd_before = jnp.where(c > 0, ref[before, :].astype(F32)[halo - 1:halo], 0.0)
                even_after = jnp.where(c < half // n - 1, ref[after, :].astype(F32)[0:1], 0.0)
                odd_prev = jnp.where(row == 0, odd_before, pltpu.roll(xo, 1, 0))
                even_next = jnp.where(row == n - 1, even_after, pltpu.roll(xe, n - 1, 0))
                w0, w1, w2, b = sw_ref[0:1, cols], sw_ref[1:2, cols], sw_ref[2:3, cols], sb_ref[:, cols]
                return w0 * odd_prev + w1 * xe + w2 * xo + b, w0 * xe + w1 * xo + w2 * even_next + b

            v_e, v_o = short_conv(v_ref, 0)
            x0_e, x0_o = short_conv(x0_ref, 1)
            x1_e, x1_o = short_conv(x1_ref, 2)
            u_e = v_e * x1_e
            u_o = v_o * x1_o
            ue_ref[rows, :] = u_e.astype(BF16)
            uo_ref[rows, :] = u_o.astype(BF16)
            x0e_s[rows, :] = x0_e.astype(BF16)
            x0o_s[rows, :] = x0_o.astype(BF16)
            acce_ref[rows, :] = u_e * skip_ref[...]
            acco_ref[rows, :] = u_o * skip_ref[...]
            return carry

        lax.fori_loop(0, half // n, chunk, 0, unroll=2)
        nyq_e = _dot(sg_ref[...], ue_ref[...])
        nyq_o = _dot(sg_ref[...], uo_ref[...])
        kn = kn_ref[...]
        nyq_ref[:, :e] = nyq_e * kn[:, :e] + nyq_o * kn[:, 2 * e:]
        nyq_ref[:, e:] = nyq_e * kn[:, e:2 * e] + nyq_o * kn[:, :e]

    u_e = ue_ref[...]
    u_o = uo_ref[...]
    er, ei = _dot(cr_ref[...], u_e), _dot(sr_ref[...], u_e)
    pr, pi = _dot(cr_ref[...], u_o), _dot(sr_ref[...], u_o)
    kr = kr_ref[...]
    ki = ki_ref[...]

    def spectral(xr, xi, x):
        cols = slice(x * e, (x + 1) * e)
        return xr * kr[:, cols] + xi * ki[:, cols], xi * kr[:, cols] - xr * ki[:, cols]

    a_r, a_i = spectral(er, ei, 0)
    b_r, b_i = spectral(pr, pi, 2)
    acce_ref[...] += _dot(cc_ref[...], (a_r + b_r).astype(BF16)) + _dot(sc_ref[...], (a_i + b_i).astype(BF16))
    a_r, a_i = spectral(er, ei, 1)
    b_r, b_i = spectral(pr, pi, 0)
    acco_ref[...] += _dot(cc_ref[...], (a_r + b_r).astype(BF16)) + _dot(sc_ref[...], (a_i + b_i).astype(BF16))

    @pl.when(f == pl.num_programs(1) - 1)
    def _():
        sign = (1 - 2 * (row & 1)).astype(F32)
        nyq_e = sign * nyq_ref[0:1, :e]
        nyq_o = sign * nyq_ref[0:1, e:]

        def chunk(c, carry):
            rows = pl.ds(pl.multiple_of(c * n, n), n)
            tokens = pl.ds(pl.multiple_of(2 * c * n, 2 * n), 2 * n)
            y_e = (x0e_s[rows, :].astype(F32) * (acce_ref[rows, :] + nyq_e)).astype(BF16)
            y_o = (x0o_s[rows, :].astype(F32) * (acco_ref[rows, :] + nyq_o)).astype(BF16)
            o_ref[tokens, :] = _dot(merge, jnp.concatenate([y_e, y_o], axis=0)).astype(BF16)
            return carry

        lax.fori_loop(0, half // n, chunk, 0, unroll=2)


def _hyena(p, layer, short_w, short_b, skip, tables, spectrum):
    nb, length, _ = p.shape
    half = length // 2
    cos_t, sin_t, sign_rows = tables
    kr, ki, kn = spectrum
    tf = min(512, half)
    col = lambda j: pl.BlockSpec((None, length, E_HY), lambda b, f: (b, 0, j))
    return pl.pallas_call(
        functools.partial(_hyena_kernel, half=half),
        out_shape=jax.ShapeDtypeStruct((nb, length, E_HY), BF16),
        grid=(nb, half // tf),
        in_specs=[
            col(COL_HY_V), col(COL_HY_X0), col(COL_HY_X1),
            pl.BlockSpec((None, 3, 3 * E_HY), lambda b, f: (layer, 0, 0)),
            pl.BlockSpec((None, 1, 3 * E_HY), lambda b, f: (layer, 0, 0)),
            pl.BlockSpec((None, 1, E_HY), lambda b, f: (layer, 0, 0)),
            pl.BlockSpec((SUBLANES, half), lambda b, f: (0, 0)),
            pl.BlockSpec((SUBLANES, 3 * E_HY), lambda b, f: (0, 0)),
            pl.BlockSpec((tf, half), lambda b, f: (f, 0)),
            pl.BlockSpec((tf, half), lambda b, f: (f, 0)),
            pl.BlockSpec((half, tf), lambda b, f: (0, f)),
            pl.BlockSpec((half, tf), lambda b, f: (0, f)),
            pl.BlockSpec((tf, 3 * E_HY), lambda b, f: (f, 0)),
            pl.BlockSpec((tf, 3 * E_HY), lambda b, f: (f, 0)),
        ],
        out_specs=pl.BlockSpec((None, length, E_HY), lambda b, f: (b, 0, 0)),
        scratch_shapes=[pltpu.VMEM((half, E_HY), BF16)] * 4 + [pltpu.VMEM((half, E_HY), F32)] * 2
                       + [pltpu.VMEM((SUBLANES, 2 * E_HY), F32)],
        compiler_params=_cparams("parallel", "arbitrary"),
        name="hyena",
    )(p, p, p, short_w, short_b, skip, sign_rows, kn, cos_t, sin_t, cos_t, sin_t, kr, ki)


LRU_PAD = SUBLANES
LRU_CHUNK = 128


def _lru_kernel(xc_ref, gc_ref, xl_ref, gl_ref, cw_ref, cb_ref, wg_ref, bg_ref, lam_ref, yc_ref, yl_ref,
                xpad, a_cum, b_cum, h_sum, *, tc, tl):
    off_c = LRU_PAD
    off_l = off_c + tc + LRU_PAD
    n = LRU_CHUNK
    nj = E_LRU // LANES
    n_chunks_c = tc // n
    n_chunks = (tc + tl) // n
    groups_c = tc // SUBLANES
    groups = (tc + tl) // SUBLANES

    xpad[...] = jnp.zeros_like(xpad)
    xpad[off_c:off_c + tc, :] = xc_ref[...].astype(F32)
    xpad[off_l:off_l + tl, :] = xl_ref[...].astype(F32)

    sub = lax.broadcasted_iota(jnp.int32, (n // SUBLANES, SUBLANES, LANES), 1)

    for d in range(2):
        lam = lam_ref[d:d + 1, :]
        softplus_neg = jnp.maximum(-lam, 0.0) + jnp.log1p(jnp.exp(-jnp.abs(lam)))

        def phase_a(k, carry, d=d, softplus_neg=softplus_neg):
            start = pl.multiple_of(off_c + k * n + jnp.where(k >= n_chunks_c, LRU_PAD, 0), SUBLANES)
            for j in range(nj):
                cols = slice(j * LANES, (j + 1) * LANES)
                win = xpad[pl.ds(start - LRU_PAD, n + 2 * LRU_PAD), cols]
                xc = cb_ref[d:d + 1, cols] + jnp.zeros((n, LANES), F32)
                for tap in range(LRU_CONV):
                    back = LRU_CONV - 1 - tap
                    shift = back if d == 0 else (n + 2 * LRU_PAD - back) % (n + 2 * LRU_PAD)
                    src = win if back == 0 else pltpu.roll(win, shift, 0)
                    xc = xc + cw_ref[d, tap:tap + 1, cols] * src[LRU_PAD:LRU_PAD + n]
                gates = _dot(xc.astype(BF16), wg_ref[d, j]) + bg_ref[d, j:j + 1, :]
                r = jax.nn.sigmoid(gates[:, :LANES])
                i = jax.nn.sigmoid(gates[:, LANES:])
                log_a = (-LRU_C) * r * softplus_neg[:, cols]
                a = jnp.exp(log_a)
                b = jnp.sqrt(-jnp.tanh(log_a) * (a * a + 1.0)) * (i * xc)
                a = a.reshape(n // SUBLANES, SUBLANES, LANES)
                b = b.reshape(n // SUBLANES, SUBLANES, LANES)
                for s in (1, 2, 4):
                    if d == 0:
                        keep = sub >= s
                        a_prev = pltpu.roll(a, s, 1)
                        b_prev = pltpu.roll(b, s, 1)
                    else:
                        keep = sub < SUBLANES - s
                        a_prev = pltpu.roll(a, SUBLANES - s, 1)
                        b_prev = pltpu.roll(b, SUBLANES - s, 1)
                    b = jnp.where(keep, a * b_prev + b, b)
                    a = jnp.where(keep, a * a_prev, a)
                a_cum[pl.ds(start, n), cols] = a.reshape(n, LANES)
                b_cum[pl.ds(start, n), cols] = b.reshape(n, LANES)
            return carry

        lax.fori_loop(0, n_chunks, phase_a, 0)

        def phase_b(g, h_prev, d=d):
            if d == 0:
                row = off_c + g * SUBLANES + jnp.where(g >= groups_c, LRU_PAD, 0)
            else:
                row = jnp.where(g < groups_c, off_c + (groups_c - 1 - g) * SUBLANES,
                                off_l + (groups - 1 - g) * SUBLANES)
            rows = pl.ds(pl.multiple_of(row, SUBLANES), SUBLANES)
            h = a_cum[rows, :] * h_prev + b_cum[rows, :]
            if d == 0:
                h_sum[rows, :] = h
                return jnp.broadcast_to(h[SUBLANES - 1:SUBLANES, :], (SUBLANES, E_LRU))
            h_sum[rows, :] = h_sum[rows, :] + h
            return jnp.broadcast_to(h[0:1, :], (SUBLANES, E_LRU))

        lax.fori_loop(0, groups, phase_b, jnp.zeros((SUBLANES, E_LRU), F32), unroll=4)

    yc_ref[...] = (h_sum[off_c:off_c + tc, :] * _gelu_tanh(gc_ref[...].astype(F32))).astype(BF16)

    def phase_c(k, carry):
        r0 = pl.multiple_of(k * 256, 256)
        yl_ref[pl.ds(r0, 256), :] = (h_sum[pl.ds(off_l + r0, 256), :]
                                     * _gelu_tanh(gl_ref[pl.ds(r0, 256), :].astype(F32))).astype(BF16)
        return carry

    lax.fori_loop(0, tl // 256, phase_c, 0)


def _lru_gate_weights(wr, wi, br, bi):
    per = LANES // LRU_BLOCK_DIM
    nj = E_LRU // LANES

    def lane_blocks(w):
        w = w.reshape(DEPTH, 2, nj, per, LRU_BLOCK_DIM, LRU_BLOCK_DIM)
        eye = jnp.eye(per, dtype=w.dtype)
        full = w[:, :, :, :, :, None, :] * eye[None, None, None, :, None, :, None]
        return full.reshape(DEPTH, 2, nj, LANES, LANES)

    wg = jnp.concatenate([lane_blocks(wr), lane_blocks(wi)], axis=-1).astype(BF16)
    bg = jnp.concatenate([br.reshape(DEPTH, 2, nj, LANES), bi.reshape(DEPTH, 2, nj, LANES)], axis=-1)
    return wg, bg


def _lru(p_ctx, p_lat, layer, conv_w, conv_b, wg, bg, lam):
    nb, tc, _ = p_ctx.shape
    tl = p_lat.shape[1]
    rows = tc + tl + 3 * LRU_PAD
    col = lambda t, j: pl.BlockSpec((None, t, E_LRU), lambda b: (b, 0, j))
    par = lambda shape: pl.BlockSpec((None,) + shape, lambda b: (layer,) + (0,) * len(shape))
    return pl.pallas_call(
        functools.partial(_lru_kernel, tc=tc, tl=tl),
        out_shape=[jax.ShapeDtypeStruct((nb, tc, E_LRU), BF16), jax.ShapeDtypeStruct((nb, tl, E_LRU), BF16)],
        grid=(nb,),
        in_specs=[col(tc, COL_LRU_X), col(tc, COL_LRU_G), col(tl, COL_LRU_X), col(tl, COL_LRU_G),
                  par((2, LRU_CONV, E_LRU)), par((2, E_LRU)), par((2, E_LRU // LANES, LANES, 2 * LANES)),
                  par((2, E_LRU // LANES, 2 * LANES)), par((2, E_LRU))],
        out_specs=[pl.BlockSpec((None, tc, E_LRU), lambda b: (b, 0, 0)),
                   pl.BlockSpec((None, tl, E_LRU), lambda b: (b, 0, 0))],
        scratch_shapes=[pltpu.VMEM((rows, E_LRU), F32)] * 4,
        compiler_params=_cparams("parallel"),
        name="lru",
    )(p_ctx, p_ctx, p_lat, p_lat, conv_w, conv_b, wg, bg, lam)


def _cumsum_chunk(x, sub, reverse):
    groups = x.shape[0] // SUBLANES
    x3 = x.reshape(groups, SUBLANES, x.shape[1])
    for s in (1, 2, 4):
        if reverse:
            x3 = x3 + jnp.where(sub < SUBLANES - s, pltpu.roll(x3, SUBLANES - s, 1), 0.0)
        else:
            x3 = x3 + jnp.where(sub >= s, pltpu.roll(x3, s, 1), 0.0)
    edge = 0 if reverse else SUBLANES - 1
    blocks = [None] * groups
    total = None
    for g in (range(groups - 1, -1, -1) if reverse else range(groups)):
        blk = x3[g] if total is None else x3[g] + total
        blocks[g] = blk
        total = blk[edge:edge + 1, :]
    return blocks


def _hgrn_kernel(qc_ref, ffc_ref, fbc_ref, ic_ref, ogc_ref, ql_ref, ffl_ref, fbl_ref, il_ref, ogl_ref,
                 lb_ref, ng_ref, yc_ref, yl_ref, o_acc, o_f, o_b, qg_s, kd_s, v_s, dec_s, *, layer, tc, tl):
    ch = HG_CHUNK
    raw = [lb_ref[i:i + 1, :] for i in range(DEPTH)]
    top = functools.reduce(jnp.maximum, raw)
    ex = [jnp.exp(r - top) for r in raw]
    tot = functools.reduce(lambda a, b: a + b, ex)
    lb = jnp.zeros((1, E_HG), F32)
    for i in range(1, layer + 1):
        lb = lb + ex[i] / tot

    rt = lax.broadcasted_iota(jnp.int32, (ch, ch), 0)
    ct = lax.broadcasted_iota(jnp.int32, (ch, ch), 1)
    tri = [(rt >= ct), (rt <= ct)]
    sub = lax.broadcasted_iota(jnp.int32, (ch // SUBLANES, SUBLANES, HG_DK), 1)
    n_c, n_l = tc // ch, tl // ch
    n_all = n_c + n_l
    mid_g = (ch // 2) // SUBLANES
    last = SUBLANES - 1

    def intra(q_ref, ff_ref, fb_ref, i_ref, n_chunks, chunk0):
        def body(it, carry):
            chains = []
            for c in (it * HG_TRIP + i for i in range(HG_TRIP)):
                rows = pl.ds(pl.multiple_of(c * ch, ch), ch)
                srows = pl.ds(pl.multiple_of((chunk0 + c) * ch, ch), ch)
                drows = pl.ds(pl.multiple_of((chunk0 + c) * SUBLANES, SUBLANES), SUBLANES)
                for hd in range(HG_HEADS):
                    cols = slice(hd * HG_DK, (hd + 1) * HG_DK)
                    lbh = lb[:, cols]
                    q = _silu(q_ref[rows, cols].astype(F32))
                    v = i_ref[rows, cols]
                    v_s[srows, cols] = v
                    for d, f_ref in enumerate((ff_ref, fb_ref)):
                        f = lbh + (1.0 - lbh) * jax.nn.sigmoid(f_ref[rows, cols].astype(F32))
                        k = 1.0 - f
                        blocks = _cumsum_chunk(jnp.log(f), sub, reverse=d == 1)
                        b = jnp.concatenate(blocks, axis=0)
                        if d == 0:
                            b_last, b_mid = blocks[-1][last:last + 1], blocks[mid_g][0:1]
                        else:
                            b_last, b_mid = blocks[0][0:1], blocks[mid_g - 1][last:last + 1]
                        e_mid = jnp.exp(b - b_mid)
                        qs = (q * e_mid).astype(BF16)
                        ks = (k * jnp.exp(b_mid - b)).astype(BF16)
                        qg_s[d, srows, cols] = (q * e_mid * jnp.exp(b_mid)).astype(BF16)
                        kd_s[d, srows, cols] = (k * jnp.exp(b_last - b)).astype(BF16)
                        dec_s[d, drows, cols] = jnp.broadcast_to(jnp.exp(b_last), (SUBLANES, HG_DK))
                        chains.append((srows, cols, d, qs, ks, v))
            scores = [_dot_nt(qs, ks) for _, _, _, qs, ks, _ in chains]
            masked = [jnp.where(tri[chain[2]], s, 0.0).astype(BF16) for chain, s in zip(chains, scores)]
            outs = [_dot(s, chain[5]) for chain, s in zip(chains, masked)]
            for i in range(0, len(chains), 2):
                srows, cols = chains[i][:2]
                o_acc[srows, cols] = outs[i] + outs[i + 1]
            return carry

        lax.fori_loop(0, n_chunks // HG_TRIP, body, 0)

    intra(qc_ref, ffc_ref, fbc_ref, ic_ref, n_c, 0)
    intra(ql_ref, ffl_ref, fbl_ref, il_ref, n_l, n_c)

    def inter(it, states):
        new = []
        for d in range(2):
            g = it if d == 0 else jnp.where(it < n_c, n_c - 1 - it, n_all + n_c - 1 - it)
            rows = pl.ds(pl.multiple_of(g * ch, ch), ch)
            drows = pl.ds(pl.multiple_of(g * SUBLANES, SUBLANES), SUBLANES)
            o_d = o_f if d == 0 else o_b
            for hd in range(HG_HEADS):
                cols = slice(hd * HG_DK, (hd + 1) * HG_DK)
                s_t = states[d * HG_HEADS + hd]
                o_d[rows, cols] = _dot_nt(qg_s[d, rows, cols], s_t.astype(BF16))
                dec = dec_s[d, drows, cols][0:1]
                new.append(s_t * dec + _dot_tn(v_s[rows, cols], kd_s[d, rows, cols]))
        return tuple(new)

    lax.fori_loop(0, n_all, inter, tuple(jnp.zeros((HG_DV, HG_DK), F32) for _ in range(2 * HG_HEADS)), unroll=4)

    def finish(og_ref, y_ref, base, t):
        blk = min(t, 256)

        def body(k, carry):
            r0 = pl.multiple_of(k * blk, blk)
            for hd in range(HG_HEADS):
                cols = slice(hd * HG_DV, (hd + 1) * HG_DV)
                orow = pl.ds(base + r0, blk)
                o = o_acc[orow, cols] + o_f[orow, cols] + o_b[orow, cols]
                y = _rms(o) * ng_ref[...] * _silu(og_ref[pl.ds(r0, blk), cols].astype(F32))
                y_ref[pl.ds(r0, blk), cols] = y.astype(BF16)
            return carry

        lax.fori_loop(0, t // blk, body, 0)

    finish(ogc_ref, yc_ref, 0, tc)
    finish(ogl_ref, yl_ref, tc, tl)


def _hgrn(p_ctx, p_lat, layer, lower_bounds, norm_g):
    nb, tc, _ = p_ctx.shape
    tl = p_lat.shape[1]
    t = tc + tl
    col = lambda n, j: pl.BlockSpec((None, n, E_HG), lambda b: (b, 0, j), pipeline_mode=pl.Buffered(1))
    names = (COL_HG_Q, COL_HG_FF, COL_HG_FB, COL_HG_I, COL_HG_OG)
    return pl.pallas_call(
        functools.partial(_hgrn_kernel, layer=layer, tc=tc, tl=tl),
        out_shape=[jax.ShapeDtypeStruct((nb, tc, E_HG), BF16), jax.ShapeDtypeStruct((nb, tl, E_HG), BF16)],
        grid=(nb,),
        in_specs=[col(tc, j) for j in names] + [col(tl, j) for j in names] + [
            pl.BlockSpec((DEPTH, E_HG), lambda b: (0, 0)),
            pl.BlockSpec((None, 1, HG_DV), lambda b: (layer, 0, 0)),
        ],
        out_specs=[pl.BlockSpec((None, tc, E_HG), lambda b: (b, 0, 0)),
                   pl.BlockSpec((None, tl, E_HG), lambda b: (b, 0, 0))],
        scratch_shapes=[pltpu.VMEM((t, E_HG), F32), pltpu.VMEM((t, E_HG), F32), pltpu.VMEM((t, E_HG), F32),
                        pltpu.VMEM((2, t, E_HG), BF16), pltpu.VMEM((2, t, E_HG), BF16), pltpu.VMEM((t, E_HG), BF16),
                        pltpu.VMEM((2, t // HG_CHUNK * SUBLANES, E_HG), F32)],
        compiler_params=_cparams("parallel"),
        name="hgrn",
    )(*([p_ctx] * 5 + [p_lat] * 5 + [lower_bounds, norm_g]))


def _merge_kernel(gh_ref, gl_ref, gg_ref, yh_ref, yl_ref, yg_ref, wh_ref, wl_ref, wg_ref, wo_ref,
                  h_ref, g1_ref, gain_ref, o_ref, *t_ref, n_cols, n_rows):
    def branch(g_ref, y_ref, w_ref):
        return jax.nn.sigmoid(g_ref[...].astype(F32)) * _dot(y_ref[...], w_ref[...])

    m = branch(gh_ref, yh_ref, wh_ref) + branch(gl_ref, yl_ref, wl_ref) + branch(gg_ref, yg_ref, wg_ref)
    r = _rms(_dot(m.astype(BF16), wo_ref[...])) * (gain_ref[...] * g1_ref[...])
    if n_cols:
        for j in range(D_MODEL // LANES):
            lanes = slice(j * LANES, (j + 1) * LANES)
            for c in range(n_cols):
                t_ref[0][j, pl.ds(c, n_rows, stride=n_cols), :] = r[c * n_rows:(c + 1) * n_rows, lanes]
            o_ref[:, :, lanes] = h_ref[:, :, lanes] + t_ref[0][j].reshape(n_rows, n_cols, LANES)
    else:
        o_ref[...] = h_ref[...] + r


def _merge(p, y_hy, y_lru, y_hg, w_hy, w_lru, w_hg, w_out, h, ada4, gains4, layer, *, ctx_row, col_major):
    nb, t, _ = h.shape
    tm = 512
    row_fn = (lambda b: b) if ctx_row is None else (lambda b: ctx_row)
    if col_major:
        n_rows = t // GRID_W
        n_cols = tm // n_rows
        h = h.reshape(nb, n_rows, GRID_W, D_MODEL)
        h_spec = pl.BlockSpec((None, n_rows, n_cols, D_MODEL), lambda b, i: (b, 0, i, 0))
        scratch = [pltpu.VMEM((D_MODEL // LANES, tm, LANES), F32)]
    else:
        n_rows = n_cols = 0
        h_spec = pl.BlockSpec((None, tm, D_MODEL), lambda b, i: (b, i, 0))
        scratch = []
    gate = lambda k: pl.BlockSpec((None, tm, D_MODEL), lambda b, i: (b, i, COL_GATES + k))
    y_spec = pl.BlockSpec((None, tm, E_HY), lambda b, i: (b, i, 0))
    w_spec = pl.BlockSpec((None, E_HY, D_MODEL), lambda b, i: (layer, 0, 0))
    out = pl.pallas_call(
        functools.partial(_merge_kernel, n_cols=n_cols, n_rows=n_rows),
        out_shape=jax.ShapeDtypeStruct(h.shape, F32),
        grid=(nb, t // tm),
        in_specs=[gate(0), gate(1), gate(2), y_spec, y_spec, y_spec, w_spec, w_spec, w_spec,
                  pl.BlockSpec((None, D_MODEL, D_MODEL), lambda b, i: (layer, 0, 0)),
                  h_spec, _ada_spec(layer, row_fn, 2, 2), _gain_spec(layer, 1, 2)],
        out_specs=h_spec,
        scratch_shapes=scratch,
        compiler_params=_cparams("parallel", "parallel"),
        name="merge",
    )(p, p, p, y_hy, y_lru, y_hg, w_hy, w_lru, w_hg, w_out, h, ada4, gains4)
    return out.reshape(nb, t, D_MODEL)


def _mlp_kernel(h_ref, sh_ref, sc_ref, g2_ref, gain2_ref, gain3_ref, w1_ref, w2_ref, o_ref):
    h = h_ref[...]
    u = (_rms(h) * (gain2_ref[...] * (1.0 + sc_ref[...])) + sh_ref[...]).astype(BF16)
    acc = None
    for k in range(0, D_FF, MLP_FF_BLOCK):
        a = jnp.maximum(_dot(u, w1_ref[:, k:k + MLP_FF_BLOCK]), 0.0)
        part = _dot((a * a).astype(BF16), w2_ref[k:k + MLP_FF_BLOCK, :])
        acc = part if acc is None else acc + part
    o_ref[...] = h + (g2_ref[...] * gain3_ref[...]) * _rms(acc)


def _mlp(h, ada4, gains4, w1_b, w2_b, layer, *, ctx_row):
    nb, t, _ = h.shape
    tm = 512
    row_fn = (lambda b: b) if ctx_row is None else (lambda b: ctx_row)
    h_spec = pl.BlockSpec((None, tm, D_MODEL), lambda b, i: (b, i, 0))
    return pl.pallas_call(
        _mlp_kernel,
        out_shape=jax.ShapeDtypeStruct(h.shape, F32),
        grid=(nb, t // tm),
        in_specs=[h_spec, _ada_spec(layer, row_fn, 3, 2), _ada_spec(layer, row_fn, 4, 2),
                  _ada_spec(layer, row_fn, 5, 2), _gain_spec(layer, 2, 2), _gain_spec(layer, 3, 2),
                  pl.BlockSpec((None, D_MODEL, D_FF), lambda b, i: (layer, 0, 0), pipeline_mode=pl.Buffered(1)),
                  pl.BlockSpec((None, D_FF, D_MODEL), lambda b, i: (layer, 0, 0), pipeline_mode=pl.Buffered(1))],
        out_specs=h_spec,
        compiler_params=_cparams("parallel", "parallel"),
        name="mlp",
    )(h, ada4, ada4, ada4, gains4, gains4, w1_b, w2_b)


def kernel(x, c, ctx, c_ctx, w_ada, b_ada, norm_gains, w_in, hy_short_w, hy_short_b, hy_ff_w1, hy_ff_b1,
           hy_ff_w2, hy_ff_b2, hy_ff_w3, hy_freq, hy_skip, lru_conv_w, lru_conv_b, lru_wr, lru_br, lru_wi,
           lru_bi, lru_lambda, hg_lower_bounds, hg_norm_g, w_proj_hy, w_proj_lru, w_proj_hg, w_out,
           w_mlp1, w_mlp2):
    nb, seq, _ = x.shape
    ctx_len = ctx.shape[1]
    assert seq % GRID_W == 0

    w_in_b, w_out_b = w_in.astype(BF16), w_out.astype(BF16)
    w_hy_b, w_lru_b, w_hg_b = w_proj_hy.astype(BF16), w_proj_lru.astype(BF16), w_proj_hg.astype(BF16)
    w1_b, w2_b = w_mlp1.astype(BF16), w_mlp2.astype(BF16)

    ctx_row = nb
    ada_rows = -(-(nb + 1) // SUBLANES) * SUBLANES
    cin = jnp.concatenate([c, c_ctx[None], jnp.zeros((ada_rows - nb - 1, D_MODEL), F32)], axis=0)
    ada4 = _ada_call(cin, w_ada, b_ada).reshape(DEPTH, ada_rows, 1, N_ADA * D_MODEL)
    gains4 = norm_gains.reshape(DEPTH, 4, 1, D_MODEL)

    hy_w1p = jnp.pad(hy_ff_w1, ((0, 0), (0, LANES - hy_ff_w1.shape[1]), (0, 0)))
    hy_b1 = hy_ff_b1.reshape(DEPTH, 1, HY_ORDER)
    hy_b2 = hy_ff_b2.reshape(DEPTH, 1, HY_ORDER)
    hy_sb = hy_short_b.reshape(DEPTH, 1, 3 * E_HY)
    hy_sk = hy_skip.reshape(DEPTH, 1, E_HY)
    lru_wg, lru_bg = _lru_gate_weights(lru_wr, lru_wi, lru_br, lru_bi)
    hg_ng = hg_norm_g.reshape(DEPTH, 1, HG_DV)

    tables = {seq // 2: _dft_tables(seq // 2), ctx_len // 2: _dft_tables(ctx_len // 2)}

    def hyena(p, layer):
        length = p.shape[1]
        hs, hd = _hy_taps(length, layer, hy_w1p, hy_b1, hy_ff_w2, hy_b2, hy_ff_w3, hy_freq)
        spectrum = _hy_spectrum(length // 2, *tables[length // 2], hs, hd)
        return _hyena(p, layer, hy_short_w, hy_sb, hy_sk, tables[length // 2], spectrum)

    h_lat = x
    h_ctx = ctx.reshape(1, nb * ctx_len, D_MODEL)
    for layer in range(DEPTH):
        need_ctx = layer < DEPTH - 1
        col_major = layer % 2 == 1
        p_lat = _in_proj(h_lat, ada4, gains4, w_in_b, layer, ctx_row=None, col_major=col_major)
        p_ctx_flat = _in_proj(h_ctx, ada4, gains4, w_in_b, layer, ctx_row=ctx_row, col_major=False)
        p_ctx = p_ctx_flat.reshape(nb, ctx_len, IN_WIDTH)

        y_hy_lat = hyena(p_lat, layer)
        y_lru_ctx, y_lru_lat = _lru(p_ctx, p_lat, layer, lru_conv_w, lru_conv_b, lru_wg, lru_bg, lru_lambda)
        y_hg_ctx, y_hg_lat = _hgrn(p_ctx, p_lat, layer, hg_lower_bounds, hg_ng)
        proj = (w_hy_b, w_lru_b, w_hg_b, w_out_b)
        h_lat = _merge(p_lat, y_hy_lat, y_lru_lat, y_hg_lat, *proj, h_lat, ada4, gains4, layer,
                       ctx_row=None, col_major=col_major)
        if need_ctx:
            flat = lambda y: y.reshape(1, nb * ctx_len, y.shape[-1])
            y_hy_ctx = hyena(p_ctx, layer)
            h_ctx = _merge(p_ctx_flat, flat(y_hy_ctx), flat(y_lru_ctx), flat(y_hg_ctx), *proj, h_ctx, ada4,
                           gains4, layer, ctx_row=ctx_row, col_major=False)

        h_lat = _mlp(h_lat, ada4, gains4, w1_b, w2_b, layer, ctx_row=None)
        if need_ctx:
            h_ctx = _mlp(h_ctx, ada4, gains4, w1_b, w2_b, layer, ctx_row=ctx_row)
    return h_lat
```
